```python
import jax, jax.numpy as jnp
from jax import lax
import numpy as np

D_MODEL = 1024
BATCH = 2
SEQ = 8192
DEPTH = 1

CHUNK = 64
N_MEM = 256
A_HEADS = 8
A_HEAD_DIM = 64
KV_RANK = 128
IDX_HEADS = 8
IDX_DIM = 32
TOPK_MAX = 256
Q_BLOCK = 128
CONV_CH = 512
CONV_WIDTH = 31
C_HEADS = 4
C_HEAD_DIM = 128
N_BRANCH = 3
N_EXPERTS = 32
TOP_K = 4
D_EXPERT = 1024
SWIGLU_LIMIT = 7.0
SWIGLU_ALPHA = 1.702
MOE_BLOCK = 256

LN_EPS = 1e-5
NEG_INF = -1e30
A_SCALE = A_HEAD_DIM ** -0.5
IDX_SCALE = (IDX_HEADS * IDX_DIM) ** -0.5
C_SCALE = C_HEAD_DIM ** -0.5
DN_ALPHA = (2 * DEPTH) ** 0.25
DN_BETA = (8 * DEPTH) ** -0.25

SPLITS = (A_HEADS * A_HEAD_DIM, KV_RANK, IDX_HEADS * IDX_DIM, IDX_DIM, IDX_HEADS,
          2 * CONV_CH, C_HEADS * C_HEAD_DIM, N_BRANCH * D_MODEL)
D_IN = sum(SPLITS)

kernel_name = 'hybrid_dsa_conformer_memory_moe_deepnorm'


def _layer_norm(x, g, b):
    xf = x.astype(jnp.float32)
    xc = xf - jnp.mean(xf, -1, keepdims=True)
    var = jnp.mean(xc * xc, -1, keepdims=True)
    return (xc * lax.rsqrt(var + LN_EPS) * g.astype(jnp.float32) + b.astype(jnp.float32)).astype(x.dtype)


def _rms_norm(x, g):
    xf = x.astype(jnp.float32)
    return (xf * lax.rsqrt(jnp.mean(xf * xf, -1, keepdims=True) + LN_EPS) * g.astype(jnp.float32)).astype(x.dtype)


def _split_cols(a):
    parts, start = [], 0
    for width in SPLITS:
        parts.append(a[..., start:start + width])
        start += width
    return parts


def _alibi_slopes():
    return jnp.exp2(-(8.0 / A_HEADS) * jnp.arange(1, A_HEADS + 1, dtype=jnp.float32))


def _dsa_mla_attention(q_a, c_kv, q_idx, k_idx, w_idx, w_uk, w_uv):
    bsz, seq = c_kv.shape[:2]
    topk = min(TOPK_MAX, seq // 4)
    n_blocks = seq // Q_BLOCK
    q_lat = jnp.einsum('bthd,hrd->bthr', q_a, w_uk)
    key_chunk = jnp.arange(seq) // CHUNK
    slopes = _alibi_slopes()
    k_idx32 = k_idx.astype(jnp.float32)
    c_kv32 = c_kv.astype(jnp.float32)

    def to_blocks(a):
        return jnp.moveaxis(a.reshape(bsz, n_blocks, Q_BLOCK, *a.shape[2:]), 1, 0)

    def block(args):
        qi, wi, ql, blk = args
        q_pos = blk * Q_BLOCK + jnp.arange(Q_BLOCK)
        q_chunk = q_pos // CHUNK
        admissible = key_chunk[None, :] <= q_chunk[:, None]
        rel = jax.nn.relu(jnp.einsum('bqhd,bsd->bqhs', qi.astype(jnp.float32), k_idx32))
        score = jnp.einsum('bqhs,bqh->bqs', rel, wi.astype(jnp.float32) * IDX_SCALE)
        score = jnp.where(admissible[None], score, NEG_INF)
        _, sel = lax.top_k(score, topk)
        c_sel = jax.vmap(lambda c, i: c[i])(c_kv32, sel)
        valid = key_chunk[sel] <= q_chunk[None, :, None]
        dist = jnp.abs(q_pos[None, :, None] - sel).astype(jnp.float32)
        logits = (jnp.einsum('bqhr,bqkr->bqhk', ql.astype(jnp.float32), c_sel) * A_SCALE
                  - slopes[None, None, :, None] * dist[:, :, None, :])
        logits = jnp.where(valid[:, :, None, :], logits, NEG_INF)
        p = jax.nn.softmax(logits, axis=-1)
        return jnp.einsum('bqhk,bqkr->bqhr', p, c_sel).astype(ql.dtype)

    o_lat = lax.map(block, (to_blocks(q_idx), to_blocks(w_idx), to_blocks(q_lat), jnp.arange(n_blocks)))
    o_lat = jnp.moveaxis(o_lat, 0, 1).reshape(bsz, seq, A_HEADS, KV_RANK)
    return jnp.einsum('bthr,hrd->bthd', o_lat, w_uv).reshape(bsz, seq, A_HEADS * A_HEAD_DIM)


def _conformer_conv(glu_in, w_dw, b_dw, ln_g, ln_b, w_pw2, b_pw2):
    a, g = jnp.split(glu_in, 2, axis=-1)
    u = a * jax.nn.sigmoid(g)
    u = lax.conv_general_dilated(u, w_dw[:, None, :], window_strides=(1,),
                                 padding=[(CONV_WIDTH - 1, 0)],
                                 dimension_numbers=('NWC', 'WIO', 'NWC'),
                                 feature_group_count=CONV_CH) + b_dw
    u = jax.nn.silu(_layer_norm(u, ln_g, ln_b))
    return u @ w_pw2 + b_pw2


def _memory_attention(q_c, mem, w_mem_k, w_mem_v):
    bsz, seq = q_c.shape[:2]
    n_mem = mem.shape[1]
    q = q_c.reshape(bsz, seq, C_HEADS, C_HEAD_DIM).astype(jnp.float32)
    k = (mem @ w_mem_k).reshape(bsz, n_mem, C_HEADS, C_HEAD_DIM).astype(jnp.float32)
    v = (mem @ w_mem_v).reshape(bsz, n_mem, C_HEADS, C_HEAD_DIM).astype(jnp.float32)
    p = jax.nn.softmax(jnp.einsum('bthd,bmhd->bhtm', q, k) * C_SCALE, axis=-1)
    o = jnp.einsum('bhtm,bmhd->bthd', p, v)
    return o.reshape(bsz, seq, C_HEADS * C_HEAD_DIM).astype(q_c.dtype)


def _moe(x, w_router, b_router, w1, b1, w2, b2):
    bsz, seq, dm = x.shape
    n_tok = bsz * seq
    xt = x.reshape(n_tok, dm)
    logits = (xt @ w_router + b_router).astype(jnp.float32)
    top_v, top_e = lax.top_k(logits, TOP_K)
    gate = jax.nn.softmax(top_v, axis=-1)
    n_assign = n_tok * TOP_K
    flat_e = top_e.reshape(n_assign)
    flat_tok = jnp.arange(n_assign) // TOP_K
    flat_g = gate.reshape(n_assign)
    order = jnp.argsort(flat_e)
    sorted_e = flat_e[order]
    counts = jnp.bincount(flat_e, length=N_EXPERTS)
    padded = (counts + MOE_BLOCK - 1) // MOE_BLOCK * MOE_BLOCK
    pad_end = jnp.cumsum(padded)
    pad_start = pad_end - padded
    grp_start = jnp.cumsum(counts) - counts
    dest = pad_start[sorted_e] + jnp.arange(n_assign) - grp_start[sorted_e]
    n_blocks = -(-n_assign // MOE_BLOCK) + N_EXPERTS
    n_rows = n_blocks * MOE_BLOCK
    row_tok = jnp.full((n_rows,), n_tok, dtype=jnp.int32).at[dest].set(flat_tok[order])
    row_gate = jnp.zeros((n_rows,), jnp.float32).at[dest].set(flat_g[order])
    blk_exp = jnp.minimum(jnp.searchsorted(pad_end, jnp.arange(n_blocks) * MOE_BLOCK, side='right'),
                          N_EXPERTS - 1)
    x_pad = jnp.concatenate([xt, jnp.zeros((1, dm), xt.dtype)], axis=0)
    rows = x_pad[row_tok].reshape(n_blocks, MOE_BLOCK, dm)

    def expert_block(args):
        xb, e = args
        hdn = xb @ w1[e] + b1[e]
        g, u = jnp.split(hdn, 2, axis=-1)
        g = jnp.minimum(g, SWIGLU_LIMIT)
        u = jnp.clip(u, -SWIGLU_LIMIT, SWIGLU_LIMIT)
        return ((u + 1.0) * (g * jax.nn.sigmoid(SWIGLU_ALPHA * g))) @ w2[e] + b2[e]

    y = lax.map(expert_block, (rows, blk_exp)).reshape(n_rows, dm)
    y = y * row_gate[:, None].astype(y.dtype)
    out = jax.ops.segment_sum(y, row_tok, num_segments=n_tok + 1)[:n_tok]
    return out.reshape(bsz, seq, dm)


def setup_inputs(seed: int = 0) -> dict:
    key = jax.random.key(seed)
    ks = jax.random.split(key, 32)
    L = DEPTH
    a_w = A_HEADS * A_HEAD_DIM
    c_w = C_HEADS * C_HEAD_DIM

    def nrm(k, shape, scale):
        return scale * jax.random.normal(k, shape, jnp.float32)

    return {
        'x': nrm(ks[0], (BATCH, SEQ, D_MODEL), 1.0),
        'mem': nrm(ks[1], (BATCH, N_MEM, D_MODEL), 1.0),
        'ln0_g': 1.0 + nrm(ks[2], (D_MODEL,), 0.02),
        'ln0_b': nrm(ks[3], (D_MODEL,), 0.02),
        'w_in': nrm(ks[4], (L, D_MODEL, D_IN), D_MODEL ** -0.5),
        'b_in': nrm(ks[5], (L, D_IN), 0.02),
        'kv_norm_g': 1.0 + nrm(ks[6], (L, KV_RANK), 0.02),
        'w_uk': nrm(ks[7], (L, A_HEADS, KV_RANK, A_HEAD_DIM), KV_RANK ** -0.5),
        'w_uv': nrm(ks[8], (L, A_HEADS, KV_RANK, A_HEAD_DIM), DN_BETA * KV_RANK ** -0.5),
        'w_o_a': nrm(ks[9], (L, a_w, D_MODEL), DN_BETA * a_w ** -0.5),
        'w_dw': nrm(ks[10], (L, CONV_WIDTH, CONV_CH), CONV_WIDTH ** -0.5),
        'b_dw': nrm(ks[11], (L, CONV_CH), 0.02),
        'conv_ln_g': 1.0 + nrm(ks[12], (L, CONV_CH), 0.02),
        'conv_ln_b': nrm(ks[13], (L, CONV_CH), 0.02),
        'w_pw2': nrm(ks[14], (L, CONV_CH, D_MODEL), DN_BETA * CONV_CH ** -0.5),
        'b_pw2': nrm(ks[15], (L, D_MODEL), 0.02),
        'w_mem_k': nrm(ks[16], (L, D_MODEL, c_w), D_MODEL ** -0.5),
        'w_mem_v': nrm(ks[17], (L, D_MODEL, c_w), DN_BETA * D_MODEL ** -0.5),
        'w_o_c': nrm(ks[18], (L, c_w, D_MODEL), DN_BETA * c_w ** -0.5),
        'w_out': nrm(ks[19], (L, D_MODEL, D_MODEL), DN_BETA * D_MODEL ** -0.5),
        'b_out': nrm(ks[20], (L, D_MODEL), 0.02),
        'ln1_g': 1.0 + nrm(ks[21], (L, D_MODEL), 0.02),
        'ln1_b': nrm(ks[22], (L, D_MODEL), 0.02),
        'w_router': nrm(ks[23], (L, D_MODEL, N_EXPERTS), D_MODEL ** -0.5),
        'b_router': nrm(ks[24], (L, N_EXPERTS), 0.01),
        'w1': nrm(ks[25], (L, N_EXPERTS, D_MODEL, 2 * D_EXPERT), DN_BETA * D_MODEL ** -0.5),
        'b1': nrm(ks[26], (L, N_EXPERTS, 2 * D_EXPERT), 0.01),
        'w2': nrm(ks[27], (L, N_EXPERTS, D_EXPERT, D_MODEL), DN_BETA * D_EXPERT ** -0.5),
        'b2': nrm(ks[28], (L, N_EXPERTS, D_MODEL), 0.01),
        'ln2_g': 1.0 + nrm(ks[29], (L, D_MODEL), 0.02),
        'ln2_b': nrm(ks[30], (L, D_MODEL), 0.02),
    }


def reference(x, mem, ln0_g, ln0_b, w_in, b_in, kv_norm_g, w_uk, w_uv, w_o_a,
              w_dw, b_dw, conv_ln_g, conv_ln_b, w_pw2, b_pw2,
              w_mem_k, w_mem_v, w_o_c, w_out, b_out, ln1_g, ln1_b,
              w_router, b_router, w1, b1, w2, b2, ln2_g, ln2_b):
    bsz, seq, dm = x.shape
    h = _layer_norm(x, ln0_g, ln0_b)
    for l in range(DEPTH):
        proj = h @ w_in[l] + b_in[l]
        q_a, c_kv, q_idx, k_idx, w_idx, glu_in, q_c, gate_logits = _split_cols(proj)
        c_kv = _rms_norm(c_kv, kv_norm_g[l])
        y_a = _dsa_mla_attention(q_a.reshape(bsz, seq, A_HEADS, A_HEAD_DIM), c_kv,
                                 q_idx.reshape(bsz, seq, IDX_HEADS, IDX_DIM), k_idx, w_idx,
                                 w_uk[l], w_uv[l]) @ w_o_a[l]
        y_b = _conformer_conv(glu_in, w_dw[l], b_dw[l], conv_ln_g[l], conv_ln_b[l], w_pw2[l], b_pw2[l])
        y_c = _memory_attention(q_c, mem, w_mem_k[l], w_mem_v[l]) @ w_o_c[l]
        g = jax.nn.sigmoid(gate_logits).reshape(bsz, seq, N_BRANCH, dm)
        mixed = g[:, :, 0] * y_a + g[:, :, 1] * y_b + g[:, :, 2] * y_c
        h = _layer_norm(DN_ALPHA * h + (mixed @ w_out[l] + b_out[l]), ln1_g[l], ln1_b[l])
        h = _layer_norm(DN_ALPHA * h + _moe(h, w_router[l], b_router[l], w1[l], b1[l], w2[l], b2[l]),
                        ln2_g[l], ln2_b[l])
    return h
```

```python
import functools

import jax
import jax.numpy as jnp
from jax import lax
from jax.experimental import pallas as pl
from jax.experimental.pallas import tpu as pltpu

F32 = jnp.float32
BF16 = jnp.bfloat16
I32 = jnp.int32

CHUNK = 64
A_HEADS = 8
A_HEAD_DIM = 64
KV_RANK = 128
IDX_HEADS = 8
IDX_DIM = 32
TOPK_MAX = 256
CONV_WIDTH = 31
C_HEADS = 4
C_HEAD_DIM = 128
N_BRANCH = 3
N_EXPERTS = 32
TOP_K = 4
SWIGLU_LIMIT = 7.0
SWIGLU_ALPHA = 1.702
LN_EPS = 1e-5
NEG_INF = -1e30
DEPTH = 1
DN_ALPHA = (2 * DEPTH) ** 0.25
A_SCALE = A_HEAD_DIM ** -0.5
IDX_SCALE = (IDX_HEADS * IDX_DIM) ** -0.5
C_SCALE = C_HEAD_DIM ** -0.5

LANES = 128
Q_BLOCK = 128
KEY_TILE = 512
ROW_BLOCK = 512
VMEM_LIMIT = 56 * 1024 * 1024


def _cparams(sem, vmem=VMEM_LIMIT):
    return pltpu.CompilerParams(dimension_semantics=sem, vmem_limit_bytes=vmem)


def _layer_norm(x, g, b):
    xc = x - jnp.mean(x, -1, keepdims=True)
    var = jnp.mean(xc * xc, -1, keepdims=True)
    return xc * lax.rsqrt(var + LN_EPS) * g + b


def _dot(a, b):
    return jnp.dot(a, b, preferred_element_type=F32)


def _dot_nt(a, b):
    return lax.dot_general(a, b, (((1,), (1,)), ((), ())), preferred_element_type=F32)


_P_QA = (0, 512)
_P_CKV = (512, 640)
_P_QIDX = (640, 896)
_P_KREP = (896, 1152)
_P_WIDX = (1152, 1280)
_P_GLU_A = (1280, 1792)
_P_GLU_G = (1792, 2304)
_P_QC = (2304, 2816)
_P_COLS = 2816


def _proj_kernel(x_ref, g_ref, b_ref, w_ref, bias_ref, kvg_ref,
                 h0_ref, qa_ref, ckv_ref, qidx_ref, krep_ref, widx_ref, u_ref, qc_ref):
    h = _layer_norm(x_ref[...], g_ref[...], b_ref[...])
    h0_ref[...] = h
    hb = h.astype(BF16)

    def proj(span):
        lo, hi = span
        return _dot(hb, w_ref[:, lo:hi]) + bias_ref[:, lo:hi]

    qa_ref[...] = proj(_P_QA).astype(BF16)
    ckv = proj(_P_CKV)
    ckv = ckv * lax.rsqrt(jnp.mean(ckv * ckv, -1, keepdims=True) + LN_EPS) * kvg_ref[...]
    ckv_ref[...] = ckv.astype(BF16)
    qidx_ref[...] = proj(_P_QIDX).astype(BF16)
    krep_ref[...] = proj(_P_KREP).astype(BF16)
    widx_ref[...] = proj(_P_WIDX) * IDX_SCALE
    u_ref[...] = proj(_P_GLU_A) * jax.nn.sigmoid(proj(_P_GLU_G))
    qc_ref[...] = proj(_P_QC).astype(BF16)


def _proj(xt, ln_g, ln_b, w_in, b_in, kv_g, tm=512):
    n, dm = xt.shape
    o = 0
    parts = {}
    for name, width in (("qa", 512), ("ckv", 128), ("qidx", 256), ("kidx", 32), ("widx", 8),
                        ("glu", 1024), ("qc", 512)):
        parts[name] = (o, o + width)
        o += width

    def cols(a, name):
        lo, hi = parts[name]
        return a[..., lo:hi]

    def pack(a):
        pad = jnp.zeros(a.shape[:-1] + (LANES - 8,), a.dtype)
        glu = cols(a, "glu")
        return jnp.concatenate(
            [cols(a, "qa"), cols(a, "ckv"), cols(a, "qidx"),
             jnp.concatenate([cols(a, "kidx")] * IDX_HEADS, -1),
             cols(a, "widx"), pad, glu[..., :512], glu[..., 512:], cols(a, "qc")], -1)

    w = pack(w_in).astype(BF16)
    bias = pack(b_in).reshape(1, _P_COLS)
    row = lambda i: (i, 0)
    fixed = lambda i: (0, 0)
    outs = [(dm, F32), (512, BF16), (128, BF16), (256, BF16), (256, BF16), (128, F32), (512, F32), (512, BF16)]
    return pl.pallas_call(
        _proj_kernel,
        grid=(n // tm,),
        in_specs=[pl.BlockSpec((tm, dm), row), pl.BlockSpec((1, dm), fixed), pl.BlockSpec((1, dm), fixed),
                  pl.BlockSpec((dm, _P_COLS), fixed), pl.BlockSpec((1, _P_COLS), fixed),
                  pl.BlockSpec((1, KV_RANK), fixed)],
        out_specs=[pl.BlockSpec((tm, c), row) for c, _ in outs],
        out_shape=[jax.ShapeDtypeStruct((n, c), d) for c, d in outs],
        compiler_params=_cparams(("arbitrary",)),
        name="proj",
    )(xt, ln_g.reshape(1, dm), ln_b.reshape(1, dm), w, bias, kv_g.reshape(1, KV_RANK))


def _alibi_slope(h):
    return 2.0 ** (-(8.0 / A_HEADS) * (h + 1))


def _attn_kernel(qidx_ref, widx_ref, qa_ref, krep_ref, ckv_ref, ext_ref, wuk_ref, wuv_ref, o_ref,
                 kp_ref, sc_ref, wb_ref, p_ref, acc_ref, m_ref):
    i = pl.program_id(1)
    t0 = i * Q_BLOCK

    @pl.when(i == 0)
    def _():
        kp_ref[:, :KV_RANK] = ckv_ref[0]
        kp_ref[:, KV_RANK:] = ext_ref[...]

    n_full = i // (KEY_TILE // Q_BLOCK)
    n_tiles = n_full + 1

    qi = qidx_ref[0]
    head_of_lane = lax.broadcasted_iota(I32, (1, IDX_HEADS * IDX_DIM), 1) // IDX_DIM
    qbd = jnp.concatenate([jnp.where(head_of_lane == h, qi, jnp.zeros_like(qi)) for h in range(IDX_HEADS)], 0)
    w = widx_ref[0]
    for h in range(IDX_HEADS):
        wb_ref[h] = jnp.broadcast_to(w[:, h:h + 1], (Q_BLOCK, LANES))

    row_t = t0 + lax.broadcasted_iota(I32, (Q_BLOCK, KEY_TILE), 0)
    col = lax.broadcasted_iota(I32, (Q_BLOCK, KEY_TILE), 1)

    def score_tile(j):
        kt = krep_ref[0, pl.ds(pl.multiple_of(j * KEY_TILE, KEY_TILE), KEY_TILE), :]
        s = _dot_nt(qbd, kt)
        chunks = []
        for q in range(KEY_TILE // LANES):
            acc = jnp.zeros((Q_BLOCK, LANES), F32)
            for h in range(IDX_HEADS):
                acc = acc + jnp.maximum(s[h * Q_BLOCK:(h + 1) * Q_BLOCK, q * LANES:(q + 1) * LANES], 0.0) * wb_ref[h]
            chunks.append(acc)
        return jnp.concatenate(chunks, 1) + 0.0

    def full_tile(j, c):
        sc_ref[j] = score_tile(j)
        return c

    lax.fori_loop(0, n_full, full_tile, 0)
    key_s = n_full * KEY_TILE + col
    adm_last = (key_s // CHUNK) <= (row_t // CHUNK)
    sc_ref[n_full] = jnp.where(adm_last, score_tile(n_full), -jnp.inf)

    def count_ge(cand_b):
        def body(j, cnt):
            s = sc_ref[j]
            for q in range(KEY_TILE // LANES):
                cnt = cnt + jnp.where(s[:, q * LANES:(q + 1) * LANES] >= cand_b, 1, 0)
            return cnt
        cnt = lax.fori_loop(0, n_tiles, body, jnp.zeros((Q_BLOCK, LANES), I32))
        return jnp.sum(cnt, axis=1, keepdims=True)

    def key_to_float(key):
        bits = key ^ ((key >> 31) & jnp.int32(0x7FFFFFFF))
        return lax.bitcast_convert_type(bits, F32)

    int_min = jnp.int32(-2 ** 31)

    def bit_step(b, prefix):
        step = lax.shift_left(jnp.int32(1), jnp.int32(31) - b)
        cand = prefix + step
        cand_b = jnp.broadcast_to(key_to_float(cand), (Q_BLOCK, LANES))
        return jnp.where(count_ge(cand_b) >= TOPK_MAX, cand, prefix)

    prefix = lax.fori_loop(0, 32, bit_step, jnp.full((Q_BLOCK, 1), int_min, I32))
    thr = jnp.where(prefix == int_min, -jnp.inf, key_to_float(prefix))
    thr_b = jnp.broadcast_to(thr, (Q_BLOCK, LANES))

    def count_where(pred):
        def body(j, cnt):
            s = sc_ref[j]
            for q in range(KEY_TILE // LANES):
                cnt = cnt + jnp.where(pred(s[:, q * LANES:(q + 1) * LANES], j * KEY_TILE + q * LANES), 1, 0)
            return cnt
        cnt = lax.fori_loop(0, n_tiles, body, jnp.zeros((Q_BLOCK, LANES), I32))
        return jnp.sum(cnt, axis=1, keepdims=True)

    lane128 = lax.broadcasted_iota(I32, (Q_BLOCK, LANES), 1)
    n_gt = count_where(lambda s, base: s > thr_b)
    n_eq = count_where(lambda s, base: s == thr_b)
    need = TOPK_MAX - n_gt
    seq_bits = 13

    def tie_search(_):
        def step(b, lo):
            cand = lo + lax.shift_left(jnp.int32(1), jnp.int32(seq_bits) - b)
            cand_b = jnp.broadcast_to(cand, (Q_BLOCK, LANES))
            below = count_where(lambda s, base: (s == thr_b) & ((base + lane128) < cand_b))
            return jnp.where(below < need, cand, lo)
        return lax.fori_loop(0, seq_bits + 1, step, jnp.zeros((Q_BLOCK, 1), I32))

    overflow = jnp.max(jnp.where(n_gt + n_eq > TOPK_MAX, 1, 0)) > 0
    last_tie = lax.cond(overflow, tie_search, lambda _: jnp.full((Q_BLOCK, 1), 2 ** 30, I32), 0)
    tie_b = jnp.broadcast_to(last_tie, (Q_BLOCK, LANES))

    qlat = _dot(qa_ref[0], wuk_ref[...]) * A_SCALE
    lane_e = lax.broadcasted_iota(I32, (Q_BLOCK, LANES), 1)
    t0f = t0.astype(F32)
    qp = []
    for h in range(A_HEADS):
        sl = _alibi_slope(h)
        qext = jnp.where(lane_e == 0, CHUNK * sl, jnp.where(lane_e == 1, sl, jnp.where(lane_e == 2, -sl * t0f, 0.0)))
        qp.append(jnp.concatenate([qlat[:, h * KV_RANK:(h + 1) * KV_RANK], qext], 1))
    qp = jnp.concatenate(qp, 0).astype(BF16)

    m_ref[...] = jnp.full(m_ref.shape, NEG_INF, F32)
    acc_ref[...] = jnp.zeros(acc_ref.shape, F32)

    def attend(j, last):
        s = sc_ref[j]
        kt = kp_ref[pl.ds(pl.multiple_of(j * KEY_TILE, KEY_TILE), KEY_TILE), :]
        logits = _dot_nt(qp, kt)
        mbs = []
        for q in range(KEY_TILE // LANES):
            sq = s[:, q * LANES:(q + 1) * LANES]
            kidx = j * KEY_TILE + q * LANES + lane128
            tie_ok = jnp.where(kidx <= tie_b, 0.0, NEG_INF)
            mbs.append(jnp.where(sq > thr_b, 0.0, jnp.where(sq == thr_b, tie_ok, NEG_INF)))
        mb = jnp.concatenate(mbs, 1)
        if last:
            mb = jnp.where(adm_last, mb, NEG_INF)
            ahead = jnp.maximum(key_s - row_t, 0).astype(F32)
        for h in range(A_HEADS):
            l = logits[h * Q_BLOCK:(h + 1) * Q_BLOCK] + mb
            if last:
                l = l - (2.0 * _alibi_slope(h)) * ahead
            m_old = m_ref[h]
            m_new = jnp.maximum(m_old, jnp.broadcast_to(jnp.max(l, axis=1, keepdims=True), (Q_BLOCK, LANES)))
            m_ref[h] = m_new
            alpha = jnp.exp(m_old - m_new)
            p = jnp.exp(l - jnp.concatenate([m_new] * (KEY_TILE // LANES), 1))
            p_ref[h * Q_BLOCK:(h + 1) * Q_BLOCK, :] = p.astype(BF16)
            acc_ref[h] = acc_ref[h] * jnp.concatenate([alpha, alpha], 1)
        pv = _dot(p_ref[...], kt)
        for h in range(A_HEADS):
            acc_ref[h] = acc_ref[h] + pv[h * Q_BLOCK:(h + 1) * Q_BLOCK]

    def attend_full(j, c):
        attend(j, False)
        return c

    lax.fori_loop(0, n_full, attend_full, 0)
    attend(n_full, True)

    outs = []
    for h in range(A_HEADS):
        a = acc_ref[h]
        outs.append(a[:, :KV_RANK] / a[:, KV_RANK + 2:KV_RANK + 3])
    olat = jnp.concatenate(outs, 1).astype(BF16)
    o_ref[0] = _dot(olat, wuv_ref[...]).astype(BF16)


def _dsa_attention(qidx, widx, qa, krep, ckv, w_uk, w_uv, bsz, seq):
    nq = seq // Q_BLOCK
    n_key_tiles = seq // KEY_TILE
    pos = jnp.arange(seq)
    ext = jnp.zeros((seq, LANES), F32).at[:, 0].set(pos // CHUNK).at[:, 1].set(pos % CHUNK).at[:, 2].set(1.0)
    ext = ext.astype(BF16)
    wuk_bd = jnp.zeros((A_HEADS * A_HEAD_DIM, A_HEADS * KV_RANK), F32)
    wuv_bd = jnp.zeros((A_HEADS * KV_RANK, A_HEADS * A_HEAD_DIM), F32)
    for h in range(A_HEADS):
        wuk_bd = wuk_bd.at[h * A_HEAD_DIM:(h + 1) * A_HEAD_DIM, h * KV_RANK:(h + 1) * KV_RANK].set(w_uk[h].T)
        wuv_bd = wuv_bd.at[h * KV_RANK:(h + 1) * KV_RANK, h * A_HEAD_DIM:(h + 1) * A_HEAD_DIM].set(w_uv[h])
    blk = lambda b, i: (b, i, 0)
    whole = lambda b, i: (b, 0, 0)
    fixed = lambda b, i: (0, 0)
    r3 = lambda a: a.reshape(bsz, seq, a.shape[-1])
    return pl.pallas_call(
        _attn_kernel,
        grid=(bsz, nq),
        in_specs=[pl.BlockSpec((1, Q_BLOCK, 256), blk), pl.BlockSpec((1, Q_BLOCK, LANES), blk),
                  pl.BlockSpec((1, Q_BLOCK, 512), blk),
                  pl.BlockSpec((1, seq, 256), whole), pl.BlockSpec((1, seq, KV_RANK), whole),
                  pl.BlockSpec((seq, LANES), fixed),
                  pl.BlockSpec((512, 1024), fixed), pl.BlockSpec((1024, 512), fixed)],
        out_specs=pl.BlockSpec((1, Q_BLOCK, 512), blk),
        out_shape=jax.ShapeDtypeStruct((bsz, seq, 512), BF16),
        scratch_shapes=[pltpu.VMEM((seq, 2 * KV_RANK), BF16),
                        pltpu.VMEM((n_key_tiles, Q_BLOCK, KEY_TILE), F32),
                        pltpu.VMEM((IDX_HEADS, Q_BLOCK, LANES), F32),
                        pltpu.VMEM((A_HEADS * Q_BLOCK, KEY_TILE), BF16),
                        pltpu.VMEM((A_HEADS, Q_BLOCK, 2 * KV_RANK), F32),
                        pltpu.VMEM((A_HEADS, Q_BLOCK, LANES), F32)],
        compiler_params=_cparams(("arbitrary", "arbitrary")),
        name="attn",
    )(r3(qidx), r3(widx), r3(qa), r3(krep), r3(ckv), ext, wuk_bd.astype(BF16), wuv_bd.astype(BF16))


CONV_HALO = 32


def _conv_kernel(cur_ref, prev_ref, w_ref, b_ref, g_ref, beta_ref, o_ref, win_ref):
    j = pl.program_id(1)
    tq = cur_ref.shape[1]
    halo = prev_ref[0, tq - CONV_HALO:, :]
    win_ref[:CONV_HALO] = jnp.where(j > 0, halo, jnp.zeros_like(halo))
    win_ref[CONV_HALO:] = cur_ref[0]
    acc = jnp.zeros((tq, cur_ref.shape[2]), F32)
    first = CONV_HALO - (CONV_WIDTH - 1)
    for k in range(CONV_WIDTH):
        acc = acc + win_ref[first + k:first + k + tq, :] * w_ref[k:k + 1, :]
    y = _layer_norm(acc + b_ref[...], g_ref[...], beta_ref[...])
    o_ref[0] = (y * jax.nn.sigmoid(y)).astype(BF16)


def _conv(u, w_dw, b_dw, ln_g, ln_b, bsz, seq, tq=512):
    c = u.shape[-1]
    u3 = u.reshape(bsz, seq, c)
    fixed = lambda b, j: (0, 0)
    return pl.pallas_call(
        _conv_kernel,
        grid=(bsz, seq // tq),
        in_specs=[pl.BlockSpec((1, tq, c), lambda b, j: (b, j, 0)),
                  pl.BlockSpec((1, tq, c), lambda b, j: (b, jnp.maximum(j - 1, 0), 0)),
                  pl.BlockSpec((CONV_WIDTH, c), fixed), pl.BlockSpec((1, c), fixed),
                  pl.BlockSpec((1, c), fixed), pl.BlockSpec((1, c), fixed)],
        out_specs=pl.BlockSpec((1, tq, c), lambda b, j: (b, j, 0)),
        out_shape=jax.ShapeDtypeStruct((bsz, seq, c), BF16),
        scratch_shapes=[pltpu.VMEM((tq + CONV_HALO, c), F32)],
        compiler_params=_cparams(("arbitrary", "arbitrary")),
        name="conv",
    )(u3, u3, w_dw, b_dw.reshape(1, c), ln_g.reshape(1, c), ln_b.reshape(1, c))


def _memattn_kernel(q_ref, mem_ref, wk_ref, wv_ref, o_ref, k_ref, v_ref):
    @pl.when(pl.program_id(1) == 0)
    def _():
        mb = mem_ref[0].astype(BF16)
        k_ref[...] = _dot(mb, wk_ref[...]).astype(BF16)
        v_ref[...] = _dot(mb, wv_ref[...]).astype(BF16)

    q = q_ref[0]
    outs = []
    for h in range(C_HEADS):
        sl = slice(h * C_HEAD_DIM, (h + 1) * C_HEAD_DIM)
        s = _dot_nt(q[:, sl], k_ref[:, sl]) * C_SCALE
        e = jnp.exp(s - jnp.max(s, axis=1, keepdims=True))
        p = e / jnp.sum(e, axis=1, keepdims=True)
        outs.append(_dot(p.astype(BF16), v_ref[:, sl]))
    o_ref[0] = jnp.concatenate(outs, 1).astype(BF16)


def _mem_attention(qc, mem, w_mem_k, w_mem_v, bsz, seq, tq=512):
    n_mem, dm = mem.shape[1:]
    cw = C_HEADS * C_HEAD_DIM
    fixed = lambda b, j: (0, 0)
    return pl.pallas_call(
        _memattn_kernel,
        grid=(bsz, seq // tq),
        in_specs=[pl.BlockSpec((1, tq, cw), lambda b, j: (b, j, 0)),
                  pl.BlockSpec((1, n_mem, dm), lambda b, j: (b, 0, 0)),
                  pl.BlockSpec((dm, cw), fixed), pl.BlockSpec((dm, cw), fixed)],
        out_specs=pl.BlockSpec((1, tq, cw), lambda b, j: (b, j, 0)),
        out_shape=jax.ShapeDtypeStruct((bsz, seq, cw), BF16),
        scratch_shapes=[pltpu.VMEM((n_mem, cw), BF16), pltpu.VMEM((n_mem, cw), BF16)],
        compiler_params=_cparams(("arbitrary", "arbitrary")),
        name="memattn",
    )(qc.reshape(bsz, seq, cw), mem, w_mem_k.astype(BF16), w_mem_v.astype(BF16))


def _mix_kernel(h0_ref, oa_ref, u2_ref, oc_ref, wg_ref, bg_ref, woa_ref, wpw_ref, bpw_ref, woc_ref,
                wout_ref, bout_ref, g_ref, b_ref, wr_ref, br_ref, h1_ref, rl_ref):
    h0 = h0_ref[...]
    hb = h0.astype(BF16)
    dm = h0.shape[1]
    y = (_dot(oa_ref[...], woa_ref[...]),
         _dot(u2_ref[...], wpw_ref[...]) + bpw_ref[...],
         _dot(oc_ref[...], woc_ref[...]))
    mixed = jnp.zeros_like(h0)
    for k in range(N_BRANCH):
        gate = jax.nn.sigmoid(_dot(hb, wg_ref[:, k * dm:(k + 1) * dm]) + bg_ref[:, k * dm:(k + 1) * dm])
        mixed = mixed + gate * y[k]
    out = _dot(mixed.astype(BF16), wout_ref[...]) + bout_ref[...]
    h1 = _layer_norm(DN_ALPHA * h0 + out, g_ref[...], b_ref[...])
    h1_ref[...] = h1
    rl_ref[...] = _dot(h1.astype(BF16), wr_ref[...]) + br_ref[...]


def _mix(h0, oa, u2, oc, w_gate, b_gate, w_o_a, w_pw2, b_pw2, w_o_c, w_out, b_out, ln_g, ln_b, w_router, b_router,
         tm=256):
    n, dm = h0.shape
    row = lambda i: (i, 0)
    fixed = lambda i: (0, 0)
    wr = jnp.zeros((dm, LANES), F32).at[:, :N_EXPERTS].set(w_router).astype(BF16)
    br = jnp.zeros((1, LANES), F32).at[0, :N_EXPERTS].set(b_router)
    full = lambda a: pl.BlockSpec(a.shape, fixed)
    ws = [w_gate.astype(BF16), b_gate.reshape(1, -1), w_o_a.astype(BF16), w_pw2.astype(BF16), b_pw2.reshape(1, -1),
          w_o_c.astype(BF16), w_out.astype(BF16), b_out.reshape(1, -1), ln_g.reshape(1, -1), ln_b.reshape(1, -1),
          wr, br]
    return pl.pallas_call(
        _mix_kernel,
        grid=(n // tm,),
        in_specs=[pl.BlockSpec((tm, dm), row), pl.BlockSpec((tm, 512), row), pl.BlockSpec((tm, 512), row),
                  pl.BlockSpec((tm, 512), row)] + [full(a) for a in ws],
        out_specs=[pl.BlockSpec((tm, dm), row), pl.BlockSpec((tm, LANES), row)],
        out_shape=[jax.ShapeDtypeStruct((n, dm), F32), jax.ShapeDtypeStruct((n, LANES), F32)],
        compiler_params=_cparams(("arbitrary",)),
        name="mix",
    )(h0, oa, u2, oc, *ws)


def _route_kernel(rl_ref, eid_ref, gate_ref, rank_ref, cnt_ref, carry_ref):
    @pl.when(pl.program_id(0) == 0)
    def _():
        carry_ref[...] = jnp.zeros_like(carry_ref)

    tm = rl_ref.shape[0]
    lane = lax.broadcasted_iota(I32, (tm, LANES), 1)
    vals = jnp.where(lane < N_EXPERTS, rl_ref[...], -jnp.inf)
    top_v, top_i, sels = [], [], []
    for _ in range(TOP_K):
        m = jnp.max(vals, axis=1, keepdims=True)
        idx = jnp.min(jnp.where(vals == m, lane, LANES), axis=1, keepdims=True)
        sel = lane == idx
        top_v.append(m)
        top_i.append(idx)
        sels.append(sel)
        vals = jnp.where(sel, -jnp.inf, vals)
    es = [jnp.exp(v - top_v[0]) for v in top_v]
    denom = es[0] + es[1] + es[2] + es[3]
    onehot = jnp.zeros((tm, LANES), F32)
    for sel in sels:
        onehot = onehot + jnp.where(sel, 1.0, 0.0)
    r = lax.broadcasted_iota(I32, (tm, tm), 0)
    c = lax.broadcasted_iota(I32, (tm, tm), 1)
    tri = jnp.where(c < r, 1.0, 0.0).astype(BF16)
    before = _dot(tri, onehot.astype(BF16)) + carry_ref[...]
    carry = carry_ref[...] + jnp.sum(onehot, axis=0, keepdims=True)
    carry_ref[...] = carry
    cnt_ref[...] = carry
    eid = jnp.zeros((tm, LANES), I32)
    gate = jnp.zeros((tm, LANES), F32)
    rank = jnp.zeros((tm, LANES), F32)
    for k in range(TOP_K):
        eid = jnp.where(lane == k, top_i[k], eid)
        gate = jnp.where(lane == k, es[k] / denom, gate)
        rank = jnp.where(lane == k, jnp.sum(jnp.where(sels[k], before, 0.0), axis=1, keepdims=True), rank)
    eid_ref[...] = eid
    gate_ref[...] = gate
    rank_ref[...] = rank.astype(I32)


def _route(rl, tm=512):
    n = rl.shape[0]
    row = lambda i: (i, 0)
    return pl.pallas_call(
        _route_kernel,
        grid=(n // tm,),
        in_specs=[pl.BlockSpec((tm, LANES), row)],
        out_specs=[pl.BlockSpec((tm, LANES), row), pl.BlockSpec((tm, LANES), row), pl.BlockSpec((tm, LANES), row),
                   pl.BlockSpec((1, LANES), lambda i: (0, 0))],
        out_shape=[jax.ShapeDtypeStruct((n, LANES), I32), jax.ShapeDtypeStruct((n, LANES), F32),
                   jax.ShapeDtypeStruct((n, LANES), I32), jax.ShapeDtypeStruct((1, LANES), F32)],
        scratch_shapes=[pltpu.VMEM((1, LANES), F32)],
        compiler_params=_cparams(("arbitrary",)),
        name="route",
    )(rl)


def _row_copy(src_hbm, src_row, dst_ref, dst_row, sem):
    return pltpu.make_async_copy(src_hbm.at[pl.ds(src_row, 1)], dst_ref.at[pl.ds(dst_row, 1)], sem)


def _dispatch_kernel(dest_ref, h1_hbm, xs_in_hbm, xs_hbm, sem):
    del xs_in_hbm
    n_copies = dest_ref.shape[2]
    tok0 = pl.program_id(0) * (n_copies // TOP_K)

    def start(r, c):
        _row_copy(h1_hbm, tok0 + r // TOP_K, xs_hbm, dest_ref[0, 0, r], sem).start()
        return c

    def wait(r, c):
        _row_copy(h1_hbm, 0, xs_hbm, 0, sem).wait()
        return c

    lax.fori_loop(0, n_copies, start, 0)
    lax.fori_loop(0, n_copies, wait, 0)


def _dispatch(h1, dest, n_rows, tm=256):
    n, dm = h1.shape
    steps = n // tm
    dest3 = dest.reshape(steps, 1, tm * TOP_K)
    xs0 = jnp.zeros((n_rows, dm), F32)
    return pl.pallas_call(
        _dispatch_kernel,
        grid=(steps,),
        in_specs=[pl.BlockSpec((1, 1, tm * TOP_K), lambda i: (i, 0, 0), memory_space=pltpu.SMEM),
                  pl.BlockSpec(memory_space=pl.ANY), pl.BlockSpec(memory_space=pl.ANY)],
        out_specs=pl.BlockSpec(memory_space=pl.ANY),
        out_shape=jax.ShapeDtypeStruct((n_rows, dm), F32),
        scratch_shapes=[pltpu.SemaphoreType.DMA(())],
        input_output_aliases={2: 0},
        compiler_params=_cparams(("arbitrary",)),
        name="dispatch",
    )(dest3, h1, xs0)


def _experts_kernel(blk_exp_ref, n_used_ref, xs_ref, w1_ref, b1_ref, w2_ref, b2_ref, ys_ref, w1b_ref, w2b_ref):
    b = pl.program_id(0)
    prev = blk_exp_ref[jnp.maximum(b - 1, 0)]
    used = b < n_used_ref[0]

    @pl.when(used & ((b == 0) | (blk_exp_ref[b] != prev)))
    def _():
        w1b_ref[...] = w1_ref[0].astype(BF16)
        w2b_ref[...] = w2_ref[0].astype(BF16)

    @pl.when(used)
    def _():
        d_exp = w2b_ref.shape[0]
        hdn = _dot(xs_ref[...].astype(BF16), w1b_ref[...]) + b1_ref[0]
        g = jnp.minimum(hdn[:, :d_exp], SWIGLU_LIMIT)
        u = jnp.clip(hdn[:, d_exp:], -SWIGLU_LIMIT, SWIGLU_LIMIT)
        act = (u + 1.0) * (g * jax.nn.sigmoid(SWIGLU_ALPHA * g))
        ys_ref[...] = _dot(act.astype(BF16), w2b_ref[...]) + b2_ref[0]

    @pl.when(jnp.logical_not(used))
    def _():
        ys_ref[...] = jnp.zeros_like(ys_ref)


def _experts(xs, blk_exp, n_used, w1, b1, w2, b2):
    n_rows, dm = xs.shape
    n_exp, _, two_f = w1.shape
    d_exp = two_f // 2
    grid_spec = pltpu.PrefetchScalarGridSpec(
        num_scalar_prefetch=2,
        grid=(n_rows // ROW_BLOCK,),
        in_specs=[pl.BlockSpec((ROW_BLOCK, dm), lambda b, be, nu: (b, 0)),
                  pl.BlockSpec((1, dm, two_f), lambda b, be, nu: (be[b], 0, 0)),
                  pl.BlockSpec((1, 1, two_f), lambda b, be, nu: (be[b], 0, 0)),
                  pl.BlockSpec((1, d_exp, dm), lambda b, be, nu: (be[b], 0, 0)),
                  pl.BlockSpec((1, 1, dm), lambda b, be, nu: (be[b], 0, 0))],
        out_specs=pl.BlockSpec((ROW_BLOCK, dm), lambda b, be, nu: (b, 0)),
        scratch_shapes=[pltpu.VMEM((dm, two_f), BF16), pltpu.VMEM((d_exp, dm), BF16)],
    )
    return pl.pallas_call(
        _experts_kernel,
        grid_spec=grid_spec,
        out_shape=jax.ShapeDtypeStruct((n_rows, dm), F32),
        compiler_params=_cparams(("arbitrary",)),
        name="experts",
    )(blk_exp, n_used, xs, w1, b1.reshape(n_exp, 1, two_f), w2, b2.reshape(n_exp, 1, dm))


def _combine_kernel(dest_ref, ys_hbm, h1_ref, gate_ref, g_ref, b_ref, o_ref, buf_ref, sem):
    tm = h1_ref.shape[0]
    n_copies = tm * TOP_K

    def start(r, c):
        _row_copy(ys_hbm, dest_ref[0, 0, r], buf_ref.at[r % TOP_K], r // TOP_K, sem).start()
        return c

    def wait(r, c):
        _row_copy(ys_hbm, 0, buf_ref.at[0], 0, sem).wait()
        return c

    lax.fori_loop(0, n_copies, start, 0)
    lax.fori_loop(0, n_copies, wait, 0)
    gate = gate_ref[...]
    moe = jnp.zeros(h1_ref.shape, F32)
    for k in range(TOP_K):
        moe = moe + buf_ref[k] * gate[:, k:k + 1]
    o_ref[...] = _layer_norm(DN_ALPHA * h1_ref[...] + moe, g_ref[...], b_ref[...])


def _combine(ys, dest, h1, gate, ln_g, ln_b, tm=128):
    n, dm = h1.shape
    steps = n // tm
    dest3 = dest.reshape(steps, 1, tm * TOP_K)
    row = lambda i: (i, 0)
    fixed = lambda i: (0, 0)
    return pl.pallas_call(
        _combine_kernel,
        grid=(steps,),
        in_specs=[pl.BlockSpec((1, 1, tm * TOP_K), lambda i: (i, 0, 0), memory_space=pltpu.SMEM),
                  pl.BlockSpec(memory_space=pl.ANY),
                  pl.BlockSpec((tm, dm), row), pl.BlockSpec((tm, LANES), row),
                  pl.BlockSpec((1, dm), fixed), pl.BlockSpec((1, dm), fixed)],
        out_specs=pl.BlockSpec((tm, dm), row),
        out_shape=jax.ShapeDtypeStruct((n, dm), F32),
        scratch_shapes=[pltpu.VMEM((TOP_K, tm, dm), F32), pltpu.SemaphoreType.DMA(())],
        compiler_params=_cparams(("arbitrary",)),
        name="combine",
    )(dest3, ys, h1, gate, ln_g.reshape(1, dm), ln_b.reshape(1, dm))


def _moe(h1, rl, w1, b1, w2, b2, ln_g, ln_b):
    n, dm = h1.shape
    eid, gate, rank, counts = _route(rl)
    eid = eid[:, :TOP_K]
    rank = rank[:, :TOP_K]
    counts = counts[0, :N_EXPERTS].astype(I32)
    padded = (counts + ROW_BLOCK - 1) // ROW_BLOCK * ROW_BLOCK
    pad_end = jnp.cumsum(padded)
    pad_start = pad_end - padded
    dest = (pad_start[eid] + rank).reshape(n * TOP_K)
    n_blocks = n * TOP_K // ROW_BLOCK + N_EXPERTS
    n_used = (pad_end[-1] // ROW_BLOCK).astype(I32)
    blk = jnp.minimum(jnp.arange(n_blocks, dtype=I32), n_used - 1)
    blk_exp = jnp.minimum(jnp.searchsorted(pad_end, blk * ROW_BLOCK, side="right"), N_EXPERTS - 1).astype(I32)
    xs = _dispatch(h1, dest, n_blocks * ROW_BLOCK)
    ys = _experts(xs, blk_exp, n_used.reshape(1), w1, b1, w2, b2)
    return _combine(ys, dest, h1, gate, ln_g, ln_b)


def kernel(x, mem, ln0_g, ln0_b, w_in, b_in, kv_norm_g, w_uk, w_uv, w_o_a, w_dw, b_dw, conv_ln_g, conv_ln_b, w_pw2, b_pw2, w_mem_k, w_mem_v, w_o_c, w_out, b_out, ln1_g, ln1_b, w_router, b_router, w1, b1, w2, b2, ln2_g, ln2_b):
    bsz, seq, dm = x.shape
    n = bsz * seq
    assert w_in.shape[0] == DEPTH == 1, "the stem LayerNorm is fused into the single layer's projection"
    l = 0
    gate_lo = w_in.shape[-1] - N_BRANCH * dm
    h0, qa, ckv, qidx, krep, widx, u, qc = _proj(x.reshape(n, dm), ln0_g, ln0_b, w_in[l, :, :gate_lo],
                                                   b_in[l, :gate_lo], kv_norm_g[l])
    oa = _dsa_attention(qidx, widx, qa, krep, ckv, w_uk[l], w_uv[l], bsz, seq).reshape(n, -1)
    u2 = _conv(u, w_dw[l], b_dw[l], conv_ln_g[l], conv_ln_b[l], bsz, seq).reshape(n, -1)
    oc = _mem_attention(qc, mem, w_mem_k[l], w_mem_v[l], bsz, seq).reshape(n, -1)
    h1, rl = _mix(h0, oa, u2, oc, w_in[l, :, gate_lo:], b_in[l, gate_lo:], w_o_a[l], w_pw2[l], b_pw2[l],
                  w_o_c[l], w_out[l], b_out[l], ln1_g[l], ln1_b[l], w_router[l], b_router[l])
    h2 = _moe(h1, rl, w1[l], b1[l], w2[l], b2[l], ln2_g[l], ln2_b[l])
    return h2.reshape(bsz, seq, dm)
```

```python
import functools

import jax
import jax.numpy as jnp
from jax import lax
from jax.experimental import pallas as pl
from jax.experimental.pallas import tpu as pltpu

F32 = jnp.float32
BF16 = jnp.bfloat16
I32 = jnp.int32

CHUNK = 64
A_HEADS = 8
A_HEAD_DIM = 64
KV_RANK = 128
IDX_HEADS = 8
IDX_DIM = 32
TOPK_MAX = 256
CONV_WIDTH = 31
C_HEADS = 4
C_HEAD_DIM = 128
N_BRANCH = 3
N_EXPERTS = 32
TOP_K = 4
SWIGLU_LIMIT = 7.0
SWIGLU_ALPHA = 1.702
LN_EPS = 1e-5
NEG_INF = -1e30
DEPTH = 1
DN_ALPHA = (2 * DEPTH) ** 0.25
A_SCALE = A_HEAD_DIM ** -0.5
IDX_SCALE = (IDX_HEADS * IDX_DIM) ** -0.5
C_SCALE = C_HEAD_DIM ** -0.5

LANES = 128
Q_BLOCK = 128
KEY_TILE = 512
ROW_BLOCK = 512
VMEM_LIMIT = 56 * 1024 * 1024


def _cparams(sem, vmem=VMEM_LIMIT):
    return pltpu.CompilerParams(dimension_semantics=sem, vmem_limit_bytes=vmem)


def _layer_norm(x, g, b):
    xc = x - jnp.mean(x, -1, keepdims=True)
    var = jnp.mean(xc * xc, -1, keepdims=True)
    return xc * lax.rsqrt(var + LN_EPS) * g + b


def _dot(a, b):
    return jnp.dot(a, b, preferred_element_type=F32)


def _dot_nt(a, b):
    return lax.dot_general(a, b, (((1,), (1,)), ((), ())), preferred_element_type=F32)


_P_QA = (0, 512)
_P_CKV = (512, 640)
_P_QIDX = (640, 896)
_P_KREP = (896, 1152)
_P_WIDX = (1152, 1280)
_P_GLU_A = (1280, 1792)
_P_GLU_G = (1792, 2304)
_P_QC = (2304, 2816)
_P_COLS = 2816


def _proj_kernel(x_ref, g_ref, b_ref, w_ref, bias_ref, kvg_ref,
                 h0_ref, qa_ref, ckv_ref, qidx_ref, krep_ref, widx_ref, u_ref, qc_ref):
    h = _layer_norm(x_ref[...], g_ref[...], b_ref[...])
    h0_ref[...] = h
    hb = h.astype(BF16)

    def proj(span):
        lo, hi = span
        return _dot(hb, w_ref[:, lo:hi]) + bias_ref[:, lo:hi]

    qa_ref[...] = proj(_P_QA).astype(BF16)
    ckv = proj(_P_CKV)
    ckv = ckv * lax.rsqrt(jnp.mean(ckv * ckv, -1, keepdims=True) + LN_EPS) * kvg_ref[...]
    ckv_ref[...] = ckv.astype(BF16)
    qidx_ref[...] = proj(_P_QIDX).astype(BF16)
    krep_ref[...] = proj(_P_KREP).astype(BF16)
    widx_ref[...] = proj(_P_WIDX) * IDX_SCALE
    u_ref[...] = proj(_P_GLU_A) * jax.nn.sigmoid(proj(_P_GLU_G))
    qc_ref[...] = proj(_P_QC).astype(BF16)


def _proj(xt, ln_g, ln_b, w_in, b_in, kv_g, tm=512):
    n, dm = xt.shape
    o = 0
    parts = {}
    for name, width in (("qa", 512), ("ckv", 128), ("qidx", 256), ("kidx", 32), ("widx", 8),
                        ("glu", 1024), ("qc", 512)):
        parts[name] = (o, o + width)
        o += width

    def cols(a, name):
        lo, hi = parts[name]
        return a[..., lo:hi]

    def pack(a):
        pad = jnp.zeros(a.shape[:-1] + (LANES - 8,), a.dtype)
        glu = cols(a, "glu")
        return jnp.concatenate(
            [cols(a, "qa"), cols(a, "ckv"), cols(a, "qidx"),
             jnp.concatenate([cols(a, "kidx")] * IDX_HEADS, -1),
             cols(a, "widx"), pad, glu[..., :512], glu[..., 512:], cols(a, "qc")], -1)

    w = pack(w_in).astype(BF16)
    bias = pack(b_in).reshape(1, _P_COLS)
    row = lambda i: (i, 0)
    fixed = lambda i: (0, 0)
    outs = [(dm, F32), (512, BF16), (128, BF16), (256, BF16), (256, BF16), (128, F32), (512, F32), (512, BF16)]
    return pl.pallas_call(
        _proj_kernel,
        grid=(n // tm,),
        in_specs=[pl.BlockSpec((tm, dm), row), pl.BlockSpec((1, dm), fixed), pl.BlockSpec((1, dm), fixed),
                  pl.BlockSpec((dm, _P_COLS), fixed), pl.BlockSpec((1, _P_COLS), fixed),
                  pl.BlockSpec((1, KV_RANK), fixed)],
        out_specs=[pl.BlockSpec((tm, c), row) for c, _ in outs],
        out_shape=[jax.ShapeDtypeStruct((n, c), d) for c, d in outs],
        compiler_params=_cparams(("arbitrary",)),
        name="proj",
    )(xt, ln_g.reshape(1, dm), ln_b.reshape(1, dm), w, bias, kv_g.reshape(1, KV_RANK))


def _alibi_slope(h):
    return 2.0 ** (-(8.0 / A_HEADS) * (h + 1))


def _attn_kernel(qidx_ref, widx_ref, qa_ref, krep_ref, ckv_ref, ext_ref, wuk_ref, wuv_ref, o_ref,
                 kp_ref, sc_ref, wb_ref, p_ref, acc_ref, m_ref):
    i = pl.program_id(1)
    t0 = i * Q_BLOCK

    @pl.when(i == 0)
    def _():
        kp_ref[:, :KV_RANK] = ckv_ref[0]
        kp_ref[:, KV_RANK:] = ext_ref[...]

    n_full = i // (KEY_TILE // Q_BLOCK)
    n_tiles = n_full + 1

    qi = qidx_ref[0]
    head_of_lane = lax.broadcasted_iota(I32, (1, IDX_HEADS * IDX_DIM), 1) // IDX_DIM
    qbd = jnp.concatenate([jnp.where(head_of_lane == h, qi, jnp.zeros_like(qi)) for h in range(IDX_HEADS)], 0)
    w = widx_ref[0]
    for h in range(IDX_HEADS):
        wb_ref[h] = jnp.broadcast_to(w[:, h:h + 1], (Q_BLOCK, LANES))

    row_t = t0 + lax.broadcasted_iota(I32, (Q_BLOCK, KEY_TILE), 0)
    col = lax.broadcasted_iota(I32, (Q_BLOCK, KEY_TILE), 1)

    def score_tile(j):
        kt = krep_ref[0, pl.ds(pl.multiple_of(j * KEY_TILE, KEY_TILE), KEY_TILE), :]
        s = _dot_nt(qbd, kt)
        chunks = []
        for q in range(KEY_TILE // LANES):
            acc = jnp.zeros((Q_BLOCK, LANES), F32)
            for h in range(IDX_HEADS):
                acc = acc + jnp.maximum(s[h * Q_BLOCK:(h + 1) * Q_BLOCK, q * LANES:(q + 1) * LANES], 0.0) * wb_ref[h]
            chunks.append(acc)
        return jnp.concatenate(chunks, 1) + 0.0

    def full_tile(j, c):
        sc_ref[j] = score_tile(j)
        return c

    lax.fori_loop(0, n_full, full_tile, 0)
    key_s = n_full * KEY_TILE + col
    adm_last = (key_s // CHUNK) <= (row_t // CHUNK)
    sc_ref[n_full] = jnp.where(adm_last, score_tile(n_full), -jnp.inf)

    def count_ge(cand_b):
        def body(j, cnt):
            s = sc_ref[j]
            for q in range(KEY_TILE // LANES):
                cnt = cnt + jnp.where(s[:, q * LANES:(q + 1) * LANES] >= cand_b, 1, 0)
            return cnt
        cnt = lax.fori_loop(0, n_tiles, body, jnp.zeros((Q_BLOCK, LANES), I32))
        return jnp.sum(cnt, axis=1, keepdims=True)

    def key_to_float(key):
        bits = key ^ ((key >> 31) & jnp.int32(0x7FFFFFFF))
        return lax.bitcast_convert_type(bits, F32)

    int_min = jnp.int32(-2 ** 31)

    def bit_step(b, prefix):
        step = lax.shift_left(jnp.int32(1), jnp.int32(31) - b)
        cand = prefix + step
        cand_b = jnp.broadcast_to(key_to_float(cand), (Q_BLOCK, LANES))
        return jnp.where(count_ge(cand_b) >= TOPK_MAX, cand, prefix)

    prefix = lax.fori_loop(0, 32, bit_step, jnp.full((Q_BLOCK, 1), int_min, I32))
    thr = jnp.where(prefix == int_min, -jnp.inf, key_to_float(prefix))
    thr_b = jnp.broadcast_to(thr, (Q_BLOCK, LANES))

    def count_where(pred):
        def body(j, cnt):
            s = sc_ref[j]
            for q in range(KEY_TILE // LANES):
                cnt = cnt + jnp.where(pred(s[:, q * LANES:(q + 1) * LANES], j * KEY_TILE + q * LANES), 1, 0)
            return cnt
        cnt = lax.fori_loop(0, n_tiles, body, jnp.zeros((Q_BLOCK, LANES), I32))
        return jnp.sum(cnt, axis=1, keepdims=True)

    lane128 = lax.broadcasted_iota(I32, (Q_BLOCK, LANES), 1)
    n_gt = count_where(lambda s, base: s > thr_b)
    n_eq = count_where(lambda s, base: s == thr_b)
    need = TOPK_MAX - n_gt
    seq_bits = 13

    def tie_search(_):
        def step(b, lo):
            cand = lo + lax.shift_left(jnp.int32(1), jnp.int32(seq_bits) - b)
            cand_b = jnp.broadcast_to(cand, (Q_BLOCK, LANES))
            below = count_where(lambda s, base: (s == thr_b) & ((base + lane128) < cand_b))
            return jnp.where(below < need, cand, lo)
        return lax.fori_loop(0, seq_bits + 1, step, jnp.zeros((Q_BLOCK, 1), I32))

    overflow = jnp.max(jnp.where(n_gt + n_eq > TOPK_MAX, 1, 0)) > 0
    last_tie = lax.cond(overflow, tie_search, lambda _: jnp.full((Q_BLOCK, 1), 2 ** 30, I32), 0)
    tie_b = jnp.broadcast_to(last_tie, (Q_BLOCK, LANES))

    qlat = _dot(qa_ref[0], wuk_ref[...]) * A_SCALE
    lane_e = lax.broadcasted_iota(I32, (Q_BLOCK, LANES), 1)
    t0f = t0.astype(F32)
    qp = []
    for h in range(A_HEADS):
        sl = _alibi_slope(h)
        qext = jnp.where(lane_e == 0, CHUNK * sl, jnp.where(lane_e == 1, sl, jnp.where(lane_e == 2, -sl * t0f, 0.0)))
        qp.append(jnp.concatenate([qlat[:, h * KV_RANK:(h + 1) * KV_RANK], qext], 1))
    qp = jnp.concatenate(qp, 0).astype(BF16)

    m_ref[...] = jnp.full(m_ref.shape, NEG_INF, F32)
    acc_ref[...] = jnp.zeros(acc_ref.shape, F32)

    def attend(j, last):
        s = sc_ref[j]
        kt = kp_ref[pl.ds(pl.multiple_of(j * KEY_TILE, KEY_TILE), KEY_TILE), :]
        logits = _dot_nt(qp, kt)
        mbs = []
        for q in range(KEY_TILE // LANES):
            sq = s[:, q * LANES:(q + 1) * LANES]
            kidx = j * KEY_TILE + q * LANES + lane128
            tie_ok = jnp.where(kidx <= tie_b, 0.0, NEG_INF)
            mbs.append(jnp.where(sq > thr_b, 0.0, jnp.where(sq == thr_b, tie_ok, NEG_INF)))
        mb = jnp.concatenate(mbs, 1)
        if last:
            mb = jnp.where(adm_last, mb, NEG_INF)
            ahead = jnp.maximum(key_s - row_t, 0).astype(F32)
        for h in range(A_HEADS):
            l = logits[h * Q_BLOCK:(h + 1) * Q_BLOCK] + mb
            if last:
                l = l - (2.0 * _alibi_slope(h)) * ahead
            m_old = m_ref[h]
            m_new = jnp.maximum(m_old, jnp.broadcast_to(jnp.max(l, axis=1, keepdims=True), (Q_BLOCK, LANES)))
            m_ref[h] = m_new
            alpha = jnp.exp(m_old - m_new)
            p = jnp.exp(l - jnp.concatenate([m_new] * (KEY_TILE // LANES), 1))
            p_ref[h * Q_BLOCK:(h + 1) * Q_BLOCK, :] = p.astype(BF16)
            acc_ref[h] = acc_ref[h] * jnp.concatenate([alpha, alpha], 1)
        pv = _dot(p_ref[...], kt)
        for h in range(A_HEADS):
            acc_ref[h] = acc_ref[h] + pv[h * Q_BLOCK:(h + 1) * Q_BLOCK]

    def attend_full(j, c):
        attend(j, False)
        return c

    lax.fori_loop(0, n_full, attend_full, 0)
    attend(n_full, True)

    outs = []
    for h in range(A_HEADS):
        a = acc_ref[h]
        outs.append(a[:, :KV_RANK] / a[:, KV_RANK + 2:KV_RANK + 3])
    olat = jnp.concatenate(outs, 1).astype(BF16)
    o_ref[0] = _dot(olat, wuv_ref[...]).astype(BF16)


def _dsa_attention(qidx, widx, qa, krep, ckv, w_uk, w_uv, bsz, seq):
    nq = seq // Q_BLOCK
    n_key_tiles = seq // KEY_TILE
    pos = lax.broadcasted_iota(I32, (seq, LANES), 0)
    lane = lax.broadcasted_iota(I32, (seq, LANES), 1)
    ext = jnp.where(lane == 0, pos // CHUNK, jnp.where(lane == 1, pos % CHUNK, jnp.where(lane == 2, 1, 0)))
    ext = ext.astype(BF16)
    eye = jnp.eye(A_HEADS, dtype=F32)
    wuk_bd = jnp.einsum("hg,hrd->hdgr", eye, w_uk).reshape(A_HEADS * A_HEAD_DIM, A_HEADS * KV_RANK)
    wuv_bd = jnp.einsum("hg,hrd->hrgd", eye, w_uv).reshape(A_HEADS * KV_RANK, A_HEADS * A_HEAD_DIM)
    blk = lambda b, i: (b, i, 0)
    whole = lambda b, i: (b, 0, 0)
    fixed = lambda b, i: (0, 0)
    r3 = lambda a: a.reshape(bsz, seq, a.shape[-1])
    return pl.pallas_call(
        _attn_kernel,
        grid=(bsz, nq),
        in_specs=[pl.BlockSpec((1, Q_BLOCK, 256), blk), pl.BlockSpec((1, Q_BLOCK, LANES), blk),
                  pl.BlockSpec((1, Q_BLOCK, 512), blk),
                  pl.BlockSpec((1, seq, 256), whole), pl.BlockSpec((1, seq, KV_RANK), whole),
                  pl.BlockSpec((seq, LANES), fixed),
                  pl.BlockSpec((512, 1024), fixed), pl.BlockSpec((1024, 512), fixed)],
        out_specs=pl.BlockSpec((1, Q_BLOCK, 512), blk),
        out_shape=jax.ShapeDtypeStruct((bsz, seq, 512), BF16),
        scratch_shapes=[pltpu.VMEM((seq, 2 * KV_RANK), BF16),
                        pltpu.VMEM((n_key_tiles, Q_BLOCK, KEY_TILE), F32),
                        pltpu.VMEM((IDX_HEADS, Q_BLOCK, LANES), F32),
                        pltpu.VMEM((A_HEADS * Q_BLOCK, KEY_TILE), BF16),
                        pltpu.VMEM((A_HEADS, Q_BLOCK, 2 * KV_RANK), F32),
                        pltpu.VMEM((A_HEADS, Q_BLOCK, LANES), F32)],
        compiler_params=_cparams(("arbitrary", "arbitrary")),
        name="attn",
    )(r3(qidx), r3(widx), r3(qa), r3(krep), r3(ckv), ext, wuk_bd.astype(BF16), wuv_bd.astype(BF16))


CONV_HALO = 32


def _conv_kernel(cur_ref, prev_ref, w_ref, b_ref, g_ref, beta_ref, o_ref, win_ref):
    j = pl.program_id(1)
    tq = cur_ref.shape[1]
    halo = prev_ref[0, tq - CONV_HALO:, :]
    win_ref[:CONV_HALO] = jnp.where(j > 0, halo, jnp.zeros_like(halo))
    win_ref[CONV_HALO:] = cur_ref[0]
    acc = jnp.zeros((tq, cur_ref.shape[2]), F32)
    first = CONV_HALO - (CONV_WIDTH - 1)
    for k in range(CONV_WIDTH):
        acc = acc + win_ref[first + k:first + k + tq, :] * w_ref[k:k + 1, :]
    y = _layer_norm(acc + b_ref[...], g_ref[...], beta_ref[...])
    o_ref[0] = (y * jax.nn.sigmoid(y)).astype(BF16)


def _conv(u, w_dw, b_dw, ln_g, ln_b, bsz, seq, tq=512):
    c = u.shape[-1]
    u3 = u.reshape(bsz, seq, c)
    fixed = lambda b, j: (0, 0)
    return pl.pallas_call(
        _conv_kernel,
        grid=(bsz, seq // tq),
        in_specs=[pl.BlockSpec((1, tq, c), lambda b, j: (b, j, 0)),
                  pl.BlockSpec((1, tq, c), lambda b, j: (b, jnp.maximum(j - 1, 0), 0)),
                  pl.BlockSpec((CONV_WIDTH, c), fixed), pl.BlockSpec((1, c), fixed),
                  pl.BlockSpec((1, c), fixed), pl.BlockSpec((1, c), fixed)],
        out_specs=pl.BlockSpec((1, tq, c), lambda b, j: (b, j, 0)),
        out_shape=jax.ShapeDtypeStruct((bsz, seq, c), BF16),
        scratch_shapes=[pltpu.VMEM((tq + CONV_HALO, c), F32)],
        compiler_params=_cparams(("arbitrary", "arbitrary")),
        name="conv",
    )(u3, u3, w_dw, b_dw.reshape(1, c), ln_g.reshape(1, c), ln_b.reshape(1, c))


def _memattn_kernel(q_ref, mem_ref, wk_ref, wv_ref, o_ref, k_ref, v_ref):
    @pl.when(pl.program_id(1) == 0)
    def _():
        mb = mem_ref[0].astype(BF16)
        k_ref[...] = _dot(mb, wk_ref[...]).astype(BF16)
        v_ref[...] = _dot(mb, wv_ref[...]).astype(BF16)

    q = q_ref[0]
    outs = []
    for h in range(C_HEADS):
        sl = slice(h * C_HEAD_DIM, (h + 1) * C_HEAD_DIM)
        s = _dot_nt(q[:, sl], k_ref[:, sl]) * C_SCALE
        e = jnp.exp(s - jnp.max(s, axis=1, keepdims=True))
        p = e / jnp.sum(e, axis=1, keepdims=True)
        outs.append(_dot(p.astype(BF16), v_ref[:, sl]))
    o_ref[0] = jnp.concatenate(outs, 1).astype(BF16)


def _mem_attention(qc, mem, w_mem_k, w_mem_v, bsz, seq, tq=512):
    n_mem, dm = mem.shape[1:]
    cw = C_HEADS * C_HEAD_DIM
    fixed = lambda b, j: (0, 0)
    return pl.pallas_call(
        _memattn_kernel,
        grid=(bsz, seq // tq),
        in_specs=[pl.BlockSpec((1, tq, cw), lambda b, j: (b, j, 0)),
                  pl.BlockSpec((1, n_mem, dm), lambda b, j: (b, 0, 0)),
                  pl.BlockSpec((dm, cw), fixed), pl.BlockSpec((dm, cw), fixed)],
        out_specs=pl.BlockSpec((1, tq, cw), lambda b, j: (b, j, 0)),
        out_shape=jax.ShapeDtypeStruct((bsz, seq, cw), BF16),
        scratch_shapes=[pltpu.VMEM((n_mem, cw), BF16), pltpu.VMEM((n_mem, cw), BF16)],
        compiler_params=_cparams(("arbitrary", "arbitrary")),
        name="memattn",
    )(qc.reshape(bsz, seq, cw), mem, w_mem_k.astype(BF16), w_mem_v.astype(BF16))


def _mix_kernel(h0_ref, oa_ref, u2_ref, oc_ref, wg_ref, bg_ref, woa_ref, wpw_ref, bpw_ref, woc_ref,
                wout_ref, bout_ref, g_ref, b_ref, wr_ref, br_ref, h1_ref, rl_ref):
    h0 = h0_ref[...]
    hb = h0.astype(BF16)
    dm = h0.shape[1]
    y = (_dot(oa_ref[...], woa_ref[...]),
         _dot(u2_ref[...], wpw_ref[...]) + bpw_ref[...],
         _dot(oc_ref[...], woc_ref[...]))
    mixed = jnp.zeros_like(h0)
    for k in range(N_BRANCH):
        gate = jax.nn.sigmoid(_dot(hb, wg_ref[:, k * dm:(k + 1) * dm]) + bg_ref[:, k * dm:(k + 1) * dm])
        mixed = mixed + gate * y[k]
    out = _dot(mixed.astype(BF16), wout_ref[...]) + bout_ref[...]
    h1 = _layer_norm(DN_ALPHA * h0 + out, g_ref[...], b_ref[...])
    h1_ref[...] = h1
    rl_ref[...] = _dot(h1.astype(BF16), wr_ref[...]) + br_ref[...]


def _mix(h0, oa, u2, oc, w_gate, b_gate, w_o_a, w_pw2, b_pw2, w_o_c, w_out, b_out, ln_g, ln_b, w_router, b_router,
         tm=256):
    n, dm = h0.shape
    row = lambda i: (i, 0)
    fixed = lambda i: (0, 0)
    wr = jnp.zeros((dm, LANES), F32).at[:, :N_EXPERTS].set(w_router).astype(BF16)
    br = jnp.zeros((1, LANES), F32).at[0, :N_EXPERTS].set(b_router)
    full = lambda a: pl.BlockSpec(a.shape, fixed)
    ws = [w_gate.astype(BF16), b_gate.reshape(1, -1), w_o_a.astype(BF16), w_pw2.astype(BF16), b_pw2.reshape(1, -1),
          w_o_c.astype(BF16), w_out.astype(BF16), b_out.reshape(1, -1), ln_g.reshape(1, -1), ln_b.reshape(1, -1),
          wr, br]
    return pl.pallas_call(
        _mix_kernel,
        grid=(n // tm,),
        in_specs=[pl.BlockSpec((tm, dm), row), pl.BlockSpec((tm, 512), row), pl.BlockSpec((tm, 512), row),
                  pl.BlockSpec((tm, 512), row)] + [full(a) for a in ws],
        out_specs=[pl.BlockSpec((tm, dm), row), pl.BlockSpec((tm, LANES), row)],
        out_shape=[jax.ShapeDtypeStruct((n, dm), F32), jax.ShapeDtypeStruct((n, LANES), F32)],
        compiler_params=_cparams(("arbitrary",)),
        name="mix",
    )(h0, oa, u2, oc, *ws)


def _route_kernel(rl_ref, eid_ref, gate_ref, rank_ref, cnt_ref, carry_ref):
    @pl.when(pl.program_id(0) == 0)
    def _():
        carry_ref[...] = jnp.zeros_like(carry_ref)

    tm = rl_ref.shape[0]
    lane = lax.broadcasted_iota(I32, (tm, LANES), 1)
    vals = jnp.where(lane < N_EXPERTS, rl_ref[...], -jnp.inf)
    top_v, top_i, sels = [], [], []
    for _ in range(TOP_K):
        m = jnp.max(vals, axis=1, keepdims=True)
        idx = jnp.min(jnp.where(vals == m, lane, LANES), axis=1, keepdims=True)
        sel = lane == idx
        top_v.append(m)
        top_i.append(idx)
        sels.append(sel)
        vals = jnp.where(sel, -jnp.inf, vals)
    es = [jnp.exp(v - top_v[0]) for v in top_v]
    denom = es[0] + es[1] + es[2] + es[3]
    onehot = jnp.zeros((tm, LANES), F32)
    for sel in sels:
        onehot = onehot + jnp.where(sel, 1.0, 0.0)
    r = lax.broadcasted_iota(I32, (tm, tm), 0)
    c = lax.broadcasted_iota(I32, (tm, tm), 1)
    tri = jnp.where(c < r, 1.0, 0.0).astype(BF16)
    before = _dot(tri, onehot.astype(BF16)) + carry_ref[...]
    carry = carry_ref[...] + jnp.sum(onehot, axis=0, keepdims=True)
    carry_ref[...] = carry
    cnt_ref[...] = carry
    eid = jnp.zeros((tm, LANES), I32)
    gate = jnp.zeros((tm, LANES), F32)
    rank = jnp.zeros((tm, LANES), F32)
    for k in range(TOP_K):
        eid = jnp.where(lane == k, top_i[k], eid)
        gate = jnp.where(lane == k, es[k] / denom, gate)
        rank = jnp.where(lane == k, jnp.sum(jnp.where(sels[k], before, 0.0), axis=1, keepdims=True), rank)
    eid_ref[...] = eid
    gate_ref[...] = gate
    rank_ref[...] = rank.astype(I32)


def _route(rl, tm=512):
    n = rl.shape[0]
    row = lambda i: (i, 0)
    return pl.pallas_call(
        _route_kernel,
        grid=(n // tm,),
        in_specs=[pl.BlockSpec((tm, LANES), row)],
        out_specs=[pl.BlockSpec((tm, LANES), row), pl.BlockSpec((tm, LANES), row), pl.BlockSpec((tm, LANES), row),
                   pl.BlockSpec((1, LANES), lambda i: (0, 0))],
        out_shape=[jax.ShapeDtypeStruct((n, LANES), I32), jax.ShapeDtypeStruct((n, LANES), F32),
                   jax.ShapeDtypeStruct((n, LANES), I32), jax.ShapeDtypeStruct((1, LANES), F32)],
        scratch_shapes=[pltpu.VMEM((1, LANES), F32)],
        compiler_params=_cparams(("arbitrary",)),
        name="route",
    )(rl)


SUBLANES = 8
DMA_UNROLL = 8


def _to_tiles(x, tile_ref):
    for s in range(SUBLANES):
        tile_ref[:, s, :] = x[:, s * LANES:(s + 1) * LANES]


def _from_tiles(tile_ref):
    return jnp.concatenate([tile_ref[:, s, :] for s in range(SUBLANES)], axis=1)


def _dispatch_kernel(dest_ref, h1_ref, xs_in_hbm, xs_hbm, stage_ref, sem):
    del xs_in_hbm
    n_copies = dest_ref.shape[2]
    _to_tiles(h1_ref[...], stage_ref)

    def start(r, c):
        pltpu.make_async_copy(stage_ref.at[r // TOP_K], xs_hbm.at[dest_ref[0, 0, r]], sem).start()
        return c

    def wait(r, c):
        pltpu.make_async_copy(stage_ref.at[0], xs_hbm.at[0], sem).wait()
        return c

    lax.fori_loop(0, n_copies, start, 0, unroll=DMA_UNROLL)
    lax.fori_loop(0, n_copies, wait, 0, unroll=DMA_UNROLL)


def _dispatch(h1, dest, n_rows, tm=256):
    n, dm = h1.shape
    steps = n // tm
    dest3 = dest.reshape(steps, 1, tm * TOP_K)
    xs0 = jnp.zeros((n_rows, SUBLANES, LANES), F32)
    return pl.pallas_call(
        _dispatch_kernel,
        grid=(steps,),
        in_specs=[pl.BlockSpec((1, 1, tm * TOP_K), lambda i: (i, 0, 0), memory_space=pltpu.SMEM),
                  pl.BlockSpec((tm, dm), lambda i: (i, 0)), pl.BlockSpec(memory_space=pl.ANY)],
        out_specs=pl.BlockSpec(memory_space=pl.ANY),
        out_shape=jax.ShapeDtypeStruct((n_rows, SUBLANES, LANES), F32),
        scratch_shapes=[pltpu.VMEM((tm, SUBLANES, LANES), F32), pltpu.SemaphoreType.DMA(())],
        input_output_aliases={2: 0},
        compiler_params=_cparams(("arbitrary",)),
        name="dispatch",
    )(dest3, h1, xs0)


def _experts_kernel(blk_exp_ref, n_used_ref, xs_ref, w1_ref, b1_ref, w2_ref, b2_ref, ys_ref, w1b_ref, w2b_ref):
    b = pl.program_id(0)
    prev = blk_exp_ref[jnp.maximum(b - 1, 0)]
    used = b < n_used_ref[0]

    @pl.when(used & ((b == 0) | (blk_exp_ref[b] != prev)))
    def _():
        w1b_ref[...] = w1_ref[0].astype(BF16)
        w2b_ref[...] = w2_ref[0].astype(BF16)

    @pl.when(used)
    def _():
        d_exp = w2b_ref.shape[0]
        hdn = _dot(_from_tiles(xs_ref).astype(BF16), w1b_ref[...]) + b1_ref[0]
        g = jnp.minimum(hdn[:, :d_exp], SWIGLU_LIMIT)
        u = jnp.clip(hdn[:, d_exp:], -SWIGLU_LIMIT, SWIGLU_LIMIT)
        act = (u + 1.0) * (g * jax.nn.sigmoid(SWIGLU_ALPHA * g))
        _to_tiles(_dot(act.astype(BF16), w2b_ref[...]) + b2_ref[0], ys_ref)

    @pl.when(jnp.logical_not(used))
    def _():
        ys_ref[...] = jnp.zeros_like(ys_ref)


def _experts(xs, blk_exp, n_used, w1, b1, w2, b2):
    n_rows = xs.shape[0]
    n_exp, dm, two_f = w1.shape
    d_exp = two_f // 2
    grid_spec = pltpu.PrefetchScalarGridSpec(
        num_scalar_prefetch=2,
        grid=(n_rows // ROW_BLOCK,),
        in_specs=[pl.BlockSpec((ROW_BLOCK, SUBLANES, LANES), lambda b, be, nu: (b, 0, 0)),
                  pl.BlockSpec((1, dm, two_f), lambda b, be, nu: (be[b], 0, 0)),
                  pl.BlockSpec((1, 1, two_f), lambda b, be, nu: (be[b], 0, 0)),
                  pl.BlockSpec((1, d_exp, dm), lambda b, be, nu: (be[b], 0, 0)),
                  pl.BlockSpec((1, 1, dm), lambda b, be, nu: (be[b], 0, 0))],
        out_specs=pl.BlockSpec((ROW_BLOCK, SUBLANES, LANES), lambda b, be, nu: (b, 0, 0)),
        scratch_shapes=[pltpu.VMEM((dm, two_f), BF16), pltpu.VMEM((d_exp, dm), BF16)],
    )
    return pl.pallas_call(
        _experts_kernel,
        grid_spec=grid_spec,
        out_shape=jax.ShapeDtypeStruct((n_rows, SUBLANES, LANES), F32),
        compiler_params=_cparams(("arbitrary",)),
        name="experts",
    )(blk_exp, n_used, xs, w1, b1.reshape(n_exp, 1, two_f), w2, b2.reshape(n_exp, 1, dm))


def _combine_kernel(dest_ref, ys_hbm, h1_ref, gate_ref, g_ref, b_ref, o_ref, buf_ref, sem):
    tm = h1_ref.shape[0]
    n_copies = tm * TOP_K

    def start(r, c):
        pltpu.make_async_copy(ys_hbm.at[dest_ref[0, 0, r]], buf_ref.at[r % TOP_K, r // TOP_K], sem).start()
        return c

    def wait(r, c):
        pltpu.make_async_copy(ys_hbm.at[0], buf_ref.at[0, 0], sem).wait()
        return c

    lax.fori_loop(0, n_copies, start, 0, unroll=DMA_UNROLL)
    lax.fori_loop(0, n_copies, wait, 0, unroll=DMA_UNROLL)
    gate = gate_ref[...]
    moe = jnp.zeros(h1_ref.shape, F32)
    for k in range(TOP_K):
        moe = moe + _from_tiles(buf_ref.at[k]) * gate[:, k:k + 1]
    o_ref[...] = _layer_norm(DN_ALPHA * h1_ref[...] + moe, g_ref[...], b_ref[...])


def _combine(ys, dest, h1, gate, ln_g, ln_b, tm=128):
    n, dm = h1.shape
    steps = n // tm
    dest3 = dest.reshape(steps, 1, tm * TOP_K)
    row = lambda i: (i, 0)
    fixed = lambda i: (0, 0)
    return pl.pallas_call(
        _combine_kernel,
        grid=(steps,),
        in_specs=[pl.BlockSpec((1, 1, tm * TOP_K), lambda i: (i, 0, 0), memory_space=pltpu.SMEM),
                  pl.BlockSpec(memory_space=pl.ANY),
                  pl.BlockSpec((tm, dm), row), pl.BlockSpec((tm, LANES), row),
                  pl.BlockSpec((1, dm), fixed), pl.BlockSpec((1, dm), fixed)],
        out_specs=pl.BlockSpec((tm, dm), row),
        out_shape=jax.ShapeDtypeStruct((n, dm), F32),
        scratch_shapes=[pltpu.VMEM((TOP_K, tm, SUBLANES, LANES), F32), pltpu.SemaphoreType.DMA(())],
        compiler_params=_cparams(("arbitrary",)),
        name="combine",
    )(dest3, ys, h1, gate, ln_g.reshape(1, dm), ln_b.reshape(1, dm))


def _moe(h1, rl, w1, b1, w2, b2, ln_g, ln_b):
    n, dm = h1.shape
    eid, gate, rank, counts = _route(rl)
    eid = eid[:, :TOP_K]
    rank = rank[:, :TOP_K]
    counts = counts[0, :N_EXPERTS].astype(I32)
    padded = (counts + ROW_BLOCK - 1) // ROW_BLOCK * ROW_BLOCK
    pad_end = jnp.cumsum(padded)
    pad_start = pad_end - padded
    dest = (pad_start[eid] + rank).reshape(n * TOP_K)
    n_blocks = n * TOP_K // ROW_BLOCK + N_EXPERTS
    n_used = (pad_end[-1] // ROW_BLOCK).astype(I32)
    blk = jnp.minimum(jnp.arange(n_blocks, dtype=I32), n_used - 1)
    blk_exp = jnp.sum((pad_end[None, :] <= (blk * ROW_BLOCK)[:, None]).astype(I32), axis=1)
    xs = _dispatch(h1, dest, n_blocks * ROW_BLOCK)
    ys = _experts(xs, blk_exp, n_used.reshape(1), w1, b1, w2, b2)
    return _combine(ys, dest, h1, gate, ln_g, ln_b)


def kernel(x, mem, ln0_g, ln0_b, w_in, b_in, kv_norm_g, w_uk, w_uv, w_o_a, w_dw, b_dw, conv_ln_g, conv_ln_b, w_pw2, b_pw2, w_mem_k, w_mem_v, w_o_c, w_out, b_out, ln1_g, ln1_b, w_router, b_router, w1, b1, w2, b2, ln2_g, ln2_b):
    bsz, seq, dm = x.shape
    n = bsz * seq
    assert w_in.shape[0] == DEPTH == 1, "the stem LayerNorm is fused into the single layer's projection"
    l = 0
    gate_lo = w_in.shape[-1] - N_BRANCH * dm
    h0, qa, ckv, qidx, krep, widx, u, qc = _proj(x.reshape(n, dm), ln0_g, ln0_b, w_in[l, :, :gate_lo],
                                                   b_in[l, :gate_lo], kv_norm_g[l])
    oa = _dsa_attention(qidx, widx, qa, krep, ckv, w_uk[l], w_uv[l], bsz, seq).reshape(n, -1)
    u2 = _conv(u, w_dw[l], b_dw[l], conv_ln_g[l], conv_ln_b[l], bsz, seq).reshape(n, -1)
    oc = _mem_attention(qc, mem, w_mem_k[l], w_mem_v[l], bsz, seq).reshape(n, -1)
    h1, rl = _mix(h0, oa, u2, oc, w_in[l, :, gate_lo:], b_in[l, gate_lo:], w_o_a[l], w_pw2[l], b_pw2[l],
                  w_o_c[l], w_out[l], b_out[l], ln1_g[l], ln1_b[l], w_router[l], b_router[l])
    h2 = _moe(h1, rl, w1[l], b1[l], w2[l], b2[l], ln2_g[l], ln2_b[l])
    return h2.reshape(bsz, seq, dm)
```

```python
import functools

import jax
import jax.numpy as jnp
from jax import lax
from jax.experimental import pallas as pl
from jax.experimental.pallas import tpu as pltpu

F32 = jnp.float32
BF16 = jnp.bfloat16
I32 = jnp.int32

CHUNK = 64
A_HEADS = 8
A_HEAD_DIM = 64
KV_RANK = 128
IDX_HEADS = 8
IDX_DIM = 32
TOPK_MAX = 256
CONV_WIDTH = 31
C_HEADS = 4
C_HEAD_DIM = 128
N_BRANCH = 3
N_EXPERTS = 32
TOP_K = 4
SWIGLU_LIMIT = 7.0
SWIGLU_ALPHA = 1.702
LN_EPS = 1e-5
NEG_INF = -1e30
DEPTH = 1
DN_ALPHA = (2 * DEPTH) ** 0.25
A_SCALE = A_HEAD_DIM ** -0.5
IDX_SCALE = (IDX_HEADS * IDX_DIM) ** -0.5
C_SCALE = C_HEAD_DIM ** -0.5

LANES = 128
Q_BLOCK = 128
KEY_TILE = 512
ATTEND_UNROLLS = (4, 2, 1)
ROW_BLOCK = 512
VMEM_LIMIT = 56 * 1024 * 1024


def _cparams(sem, vmem=VMEM_LIMIT):
    return pltpu.CompilerParams(dimension_semantics=sem, vmem_limit_bytes=vmem)


def _layer_norm(x, g, b):
    xc = x - jnp.mean(x, -1, keepdims=True)
    var = jnp.mean(xc * xc, -1, keepdims=True)
    return xc * lax.rsqrt(var + LN_EPS) * g + b


def _dot(a, b):
    return jnp.dot(a, b, preferred_element_type=F32)


def _dot_nt(a, b):
    return lax.dot_general(a, b, (((1,), (1,)), ((), ())), preferred_element_type=F32)


_P_QA = (0, 512)
_P_CKV = (512, 640)
_P_QIDX = (640, 896)
_P_KREP = (896, 1152)
_P_WIDX = (1152, 1280)
_P_GLU_A = (1280, 1792)
_P_GLU_G = (1792, 2304)
_P_QC = (2304, 2816)
_P_COLS = 2816


def _proj_kernel(x_ref, g_ref, b_ref, w_ref, bias_ref, kvg_ref,
                 h0_ref, qa_ref, ckv_ref, qidx_ref, krep_ref, widx_ref, u_ref, qc_ref):
    h = _layer_norm(x_ref[...], g_ref[...], b_ref[...])
    h0_ref[...] = h
    hb = h.astype(BF16)

    def proj(span):
        lo, hi = span
        return _dot(hb, w_ref[:, lo:hi]) + bias_ref[:, lo:hi]

    qa_ref[...] = proj(_P_QA).astype(BF16)
    ckv = proj(_P_CKV)
    ckv = ckv * lax.rsqrt(jnp.mean(ckv * ckv, -1, keepdims=True) + LN_EPS) * kvg_ref[...]
    ckv_ref[...] = ckv.astype(BF16)
    qidx_ref[...] = proj(_P_QIDX).astype(BF16)
    krep_ref[...] = proj(_P_KREP).astype(BF16)
    widx_ref[...] = proj(_P_WIDX) * IDX_SCALE
    u_ref[...] = proj(_P_GLU_A) * jax.nn.sigmoid(proj(_P_GLU_G))
    qc_ref[...] = proj(_P_QC).astype(BF16)


def _proj(xt, ln_g, ln_b, w_in, b_in, kv_g, tm=512):
    n, dm = xt.shape
    o = 0
    parts = {}
    for name, width in (("qa", 512), ("ckv", 128), ("qidx", 256), ("kidx", 32), ("widx", 8),
                        ("glu", 1024), ("qc", 512)):
        parts[name] = (o, o + width)
        o += width

    def cols(a, name):
        lo, hi = parts[name]
        return a[..., lo:hi]

    def pack(a):
        pad = jnp.zeros(a.shape[:-1] + (LANES - 8,), a.dtype)
        glu = cols(a, "glu")
        return jnp.concatenate(
            [cols(a, "qa"), cols(a, "ckv"), cols(a, "qidx"),
             jnp.concatenate([cols(a, "kidx")] * IDX_HEADS, -1),
             cols(a, "widx"), pad, glu[..., :512], glu[..., 512:], cols(a, "qc")], -1)

    w = pack(w_in).astype(BF16)
    bias = pack(b_in).reshape(1, _P_COLS)
    row = lambda i: (i, 0)
    fixed = lambda i: (0, 0)
    outs = [(dm, F32), (512, BF16), (128, BF16), (256, BF16), (256, BF16), (128, F32), (512, F32), (512, BF16)]
    return pl.pallas_call(
        _proj_kernel,
        grid=(n // tm,),
        in_specs=[pl.BlockSpec((tm, dm), row), pl.BlockSpec((1, dm), fixed), pl.BlockSpec((1, dm), fixed),
                  pl.BlockSpec((dm, _P_COLS), fixed), pl.BlockSpec((1, _P_COLS), fixed),
                  pl.BlockSpec((1, KV_RANK), fixed)],
        out_specs=[pl.BlockSpec((tm, c), row) for c, _ in outs],
        out_shape=[jax.ShapeDtypeStruct((n, c), d) for c, d in outs],
        compiler_params=_cparams(("arbitrary",)),
        name="proj",
    )(xt, ln_g.reshape(1, dm), ln_b.reshape(1, dm), w, bias, kv_g.reshape(1, KV_RANK))


def _alibi_slope(h):
    return 2.0 ** (-(8.0 / A_HEADS) * (h + 1))


def _attn_kernel(qidx_ref, widx_ref, qa_ref, krep_ref, ckv_ref, ext_ref, wuk_ref, wuv_ref, o_ref,
                 kp_ref, sc_ref, sa_ref, p_ref, acc_ref, m_ref):
    i = pl.program_id(1)
    t0 = i * Q_BLOCK

    @pl.when(i == 0)
    def _():
        kp_ref[:, :KV_RANK] = ckv_ref[0]
        kp_ref[:, KV_RANK:] = ext_ref[...]

    n_full = i // (KEY_TILE // Q_BLOCK)
    n_tiles = n_full + 1

    def key_rows(ref, j):
        return ref[pl.ds(pl.multiple_of(j * KEY_TILE, KEY_TILE), KEY_TILE), :]

    qi = qidx_ref[0]
    head_of_lane = lax.broadcasted_iota(I32, (1, IDX_HEADS * IDX_DIM), 1) // IDX_DIM
    qbd = jnp.concatenate([jnp.where(head_of_lane == h, qi, jnp.zeros_like(qi)) for h in range(IDX_HEADS)], 0)
    w_t = jnp.transpose(widx_ref[0])

    def score_tile(j):
        s = _dot_nt(key_rows(krep_ref.at[0], j), qbd)
        acc = jnp.zeros((KEY_TILE, Q_BLOCK), F32)
        for h in range(IDX_HEADS):
            acc = acc + jnp.maximum(s[:, h * Q_BLOCK:(h + 1) * Q_BLOCK], 0.0) * w_t[h:h + 1, :]
        return acc + 0.0

    def store_scores(j, s):
        sc_ref[j] = s
        bits = lax.bitcast_convert_type(s, I32)
        floor_bits = (bits + ((bits >> 31) & jnp.int32(0xFFFF))) & jnp.int32(-65536)
        sa_ref[j] = lax.bitcast_convert_type(floor_bits, F32).astype(BF16)

    def full_tile(j, c):
        store_scores(j, score_tile(j))
        return c

    lax.fori_loop(0, n_full, full_tile, 0)
    key_row = lax.broadcasted_iota(I32, (KEY_TILE, Q_BLOCK), 0)
    q_pos = t0 + lax.broadcasted_iota(I32, (KEY_TILE, Q_BLOCK), 1)
    key_last = n_full * KEY_TILE + key_row
    adm_last = (key_last // CHUNK) <= (q_pos // CHUNK)
    store_scores(n_full, jnp.where(adm_last, score_tile(n_full), -jnp.inf))

    def count_where(pred):
        def body(j, cnt):
            ind = jnp.where(pred(sc_ref[j], j * KEY_TILE), 1, 0)
            return cnt + jnp.sum(ind.reshape(KEY_TILE // 8, 8, Q_BLOCK), axis=0)
        cnt = lax.fori_loop(0, n_tiles, body, jnp.zeros((8, Q_BLOCK), I32))
        return jnp.sum(cnt, axis=0, keepdims=True)

    def key_to_float(key):
        bits = key ^ ((key >> 31) & jnp.int32(0x7FFFFFFF))
        return lax.bitcast_convert_type(bits, F32)

    def coarse_key(p):
        return (p << 16) | ((p >> 31) & jnp.int32(0xFFFF))

    def count_coarse(cand):
        def body(j, cnt):
            ind = jnp.where(sa_ref[j] >= cand, jnp.ones((), BF16), jnp.zeros((), BF16))
            parts = [ind[g * 16:(g + 1) * 16] for g in range(KEY_TILE // 16)]
            while len(parts) > 1:
                parts = [a + b for a, b in zip(parts[::2], parts[1::2])]
            return cnt + parts[0].astype(F32)
        cnt = lax.fori_loop(0, n_tiles, body, jnp.zeros((16, Q_BLOCK), F32))
        return jnp.sum(cnt, axis=0, keepdims=True)

    p_min = jnp.int32(-2 ** 15)

    def coarse_step(b, p):
        cand = p + lax.shift_left(jnp.int32(1), jnp.int32(15) - b)
        cand_f = key_to_float(coarse_key(cand)).astype(BF16)
        return jnp.where(count_coarse(cand_f) >= TOPK_MAX, cand, p)

    p16 = lax.fori_loop(0, 16, coarse_step, jnp.full((1, Q_BLOCK), p_min, I32))

    def bit_step(b, prefix):
        cand = prefix + lax.shift_left(jnp.int32(1), jnp.int32(31) - b)
        cand_f = key_to_float(cand)
        return jnp.where(count_where(lambda s, base: s >= cand_f) >= TOPK_MAX, cand, prefix)

    prefix = lax.fori_loop(16, 32, bit_step, coarse_key(p16))
    thr = jnp.where(p16 == p_min, -jnp.inf, key_to_float(prefix))

    n_gt = count_where(lambda s, base: s > thr)
    n_eq = count_where(lambda s, base: s == thr)
    need = TOPK_MAX - n_gt
    seq_bits = 13

    def tie_search(_):
        def step(b, lo):
            cand = lo + lax.shift_left(jnp.int32(1), jnp.int32(seq_bits) - b)
            below = count_where(lambda s, base: (s == thr) & ((base + key_row) < cand))
            return jnp.where(below < need, cand, lo)
        return lax.fori_loop(0, seq_bits + 1, step, jnp.zeros((1, Q_BLOCK), I32))

    overflow = jnp.max(jnp.where(n_gt + n_eq > TOPK_MAX, 1, 0)) > 0
    last_tie = lax.cond(overflow, tie_search, lambda _: jnp.full((1, Q_BLOCK), 2 ** 30, I32), 0)

    def mask_bias(j):
        s = sc_ref[j]
        tie_ok = jnp.where((j * KEY_TILE + key_row) <= last_tie, 0.0, NEG_INF)
        mb = jnp.where(s > thr, 0.0, jnp.where(s == thr, tie_ok, NEG_INF))
        return jnp.transpose(mb)

    qlat = _dot(qa_ref[0], wuk_ref[...]) * A_SCALE
    lane_e = lax.broadcasted_iota(I32, (Q_BLOCK, LANES), 1)
    t0f = t0.astype(F32)
    qp = []
    for h in range(A_HEADS):
        sl = _alibi_slope(h)
        qext = jnp.where(lane_e == 0, CHUNK * sl, jnp.where(lane_e == 1, sl, jnp.where(lane_e == 2, -sl * t0f, 0.0)))
        qp.append(jnp.concatenate([qlat[:, h * KV_RANK:(h + 1) * KV_RANK], qext], 1))
    qp = jnp.concatenate(qp, 0).astype(BF16)

    m_ref[...] = jnp.full(m_ref.shape, NEG_INF, F32)
    acc_ref[...] = jnp.zeros(acc_ref.shape, F32)

    def attend(j, slot, last=False):
        kt = key_rows(kp_ref, j)
        logits = _dot_nt(qp, kt)
        mb = mask_bias(j)
        if last:
            row_t = t0 + lax.broadcasted_iota(I32, (Q_BLOCK, KEY_TILE), 0)
            key_s = n_full * KEY_TILE + lax.broadcasted_iota(I32, (Q_BLOCK, KEY_TILE), 1)
            mb = jnp.where((key_s // CHUNK) <= (row_t // CHUNK), mb, NEG_INF)
            ahead = jnp.maximum(key_s - row_t, 0).astype(F32)
        for h in range(A_HEADS):
            l = logits[h * Q_BLOCK:(h + 1) * Q_BLOCK] + mb
            if last:
                l = l - (2.0 * _alibi_slope(h)) * ahead
            m_old = m_ref[h]
            m_new = jnp.maximum(m_old, jnp.broadcast_to(jnp.max(l, axis=1, keepdims=True), (Q_BLOCK, LANES)))
            m_ref[h] = m_new
            alpha = jnp.exp(m_old - m_new)
            p = jnp.exp(l - jnp.concatenate([m_new] * (KEY_TILE // LANES), 1))
            p_ref[slot, h * Q_BLOCK:(h + 1) * Q_BLOCK, :] = p.astype(BF16)
            acc_ref[h] = acc_ref[h] * jnp.concatenate([alpha, alpha], 1)
        pv = _dot(p_ref[slot], kt)
        for h in range(A_HEADS):
            acc_ref[h] = acc_ref[h] + pv[h * Q_BLOCK:(h + 1) * Q_BLOCK]

    def attend_group(width, first):
        def body(jj, c):
            for k in range(width):
                attend(first + width * jj + k, k % 2)
            return c
        return body

    done = 0
    for width in ATTEND_UNROLLS:
        trips = (n_full - done) // width
        lax.fori_loop(0, trips, attend_group(width, done), 0)
        done = done + trips * width
    attend(n_full, 0, last=True)

    outs = []
    for h in range(A_HEADS):
        a = acc_ref[h]
        outs.append(a[:, :KV_RANK] / a[:, KV_RANK + 2:KV_RANK + 3])
    olat = jnp.concatenate(outs, 1).astype(BF16)
    o_ref[0] = _dot(olat, wuv_ref[...]).astype(BF16)


def _dsa_attention(qidx, widx, qa, krep, ckv, w_uk, w_uv, bsz, seq):
    nq = seq // Q_BLOCK
    n_key_tiles = seq // KEY_TILE
    pos = lax.broadcasted_iota(I32, (seq, LANES), 0)
    lane = lax.broadcasted_iota(I32, (seq, LANES), 1)
    ext = jnp.where(lane == 0, pos // CHUNK, jnp.where(lane == 1, pos % CHUNK, jnp.where(lane == 2, 1, 0)))
    ext = ext.astype(BF16)
    eye = jnp.eye(A_HEADS, dtype=F32)
    wuk_bd = jnp.einsum("hg,hrd->hdgr", eye, w_uk).reshape(A_HEADS * A_HEAD_DIM, A_HEADS * KV_RANK)
    wuv_bd = jnp.einsum("hg,hrd->hrgd", eye, w_uv).reshape(A_HEADS * KV_RANK, A_HEADS * A_HEAD_DIM)
    blk = lambda b, i: (b, i, 0)
    whole = lambda b, i: (b, 0, 0)
    fixed = lambda b, i: (0, 0)
    r3 = lambda a: a.reshape(bsz, seq, a.shape[-1])
    return pl.pallas_call(
        _attn_kernel,
        grid=(bsz, nq),
        in_specs=[pl.BlockSpec((1, Q_BLOCK, 256), blk), pl.BlockSpec((1, Q_BLOCK, LANES), blk),
                  pl.BlockSpec((1, Q_BLOCK, 512), blk),
                  pl.BlockSpec((1, seq, 256), whole), pl.BlockSpec((1, seq, KV_RANK), whole),
                  pl.BlockSpec((seq, LANES), fixed),
                  pl.BlockSpec((512, 1024), fixed), pl.BlockSpec((1024, 512), fixed)],
        out_specs=pl.BlockSpec((1, Q_BLOCK, 512), blk),
        out_shape=jax.ShapeDtypeStruct((bsz, seq, 512), BF16),
        scratch_shapes=[pltpu.VMEM((seq, 2 * KV_RANK), BF16),
                        pltpu.VMEM((n_key_tiles, KEY_TILE, Q_BLOCK), F32),
                        pltpu.VMEM((n_key_tiles, KEY_TILE, Q_BLOCK), BF16),
                        pltpu.VMEM((2, A_HEADS * Q_BLOCK, KEY_TILE), BF16),
                        pltpu.VMEM((A_HEADS, Q_BLOCK, 2 * KV_RANK), F32),
                        pltpu.VMEM((A_HEADS, Q_BLOCK, LANES), F32)],
        compiler_params=_cparams(("arbitrary", "arbitrary")),
        name="attn",
    )(r3(qidx), r3(widx), r3(qa), r3(krep), r3(ckv), ext, wuk_bd.astype(BF16), wuv_bd.astype(BF16))


CONV_HALO = 32


def _conv_kernel(cur_ref, prev_ref, w_ref, b_ref, g_ref, beta_ref, o_ref, win_ref):
    j = pl.program_id(1)
    tq = cur_ref.shape[1]
    halo = prev_ref[0, tq - CONV_HALO:, :]
    win_ref[:CONV_HALO] = jnp.where(j > 0, halo, jnp.zeros_like(halo))
    win_ref[CONV_HALO:] = cur_ref[0]
    acc = jnp.zeros((tq, cur_ref.shape[2]), F32)
    first = CONV_HALO - (CONV_WIDTH - 1)
    for k in range(CONV_WIDTH):
        acc = acc + win_ref[first + k:first + k + tq, :] * w_ref[k:k + 1, :]
    y = _layer_norm(acc + b_ref[...], g_ref[...], beta_ref[...])
    o_ref[0] = (y * jax.nn.sigmoid(y)).astype(BF16)


def _conv(u, w_dw, b_dw, ln_g, ln_b, bsz, seq, tq=512):
    c = u.shape[-1]
    u3 = u.reshape(bsz, seq, c)
    fixed = lambda b, j: (0, 0)
    return pl.pallas_call(
        _conv_kernel,
        grid=(bsz, seq // tq),
        in_specs=[pl.BlockSpec((1, tq, c), lambda b, j: (b, j, 0)),
                  pl.BlockSpec((1, tq, c), lambda b, j: (b, jnp.maximum(j - 1, 0), 0)),
                  pl.BlockSpec((CONV_WIDTH, c), fixed), pl.BlockSpec((1, c), fixed),
                  pl.BlockSpec((1, c), fixed), pl.BlockSpec((1, c), fixed)],
        out_specs=pl.BlockSpec((1, tq, c), lambda b, j: (b, j, 0)),
        out_shape=jax.ShapeDtypeStruct((bsz, seq, c), BF16),
        scratch_shapes=[pltpu.VMEM((tq + CONV_HALO, c), F32)],
        compiler_params=_cparams(("arbitrary", "arbitrary")),
        name="conv",
    )(u3, u3, w_dw, b_dw.reshape(1, c), ln_g.reshape(1, c), ln_b.reshape(1, c))


def _memattn_kernel(q_ref, mem_ref, wk_ref, wv_ref, o_ref, k_ref, v_ref):
    @pl.when(pl.program_id(1) == 0)
    def _():
        mb = mem_ref[0].astype(BF16)
        k_ref[...] = _dot(mb, wk_ref[...]).astype(BF16)
        v_ref[...] = _dot(mb, wv_ref[...]).astype(BF16)

    q = q_ref[0]
    outs = []
    for h in range(C_HEADS):
        sl = slice(h * C_HEAD_DIM, (h + 1) * C_HEAD_DIM)
        s = _dot_nt(q[:, sl], k_ref[:, sl]) * C_SCALE
        e = jnp.exp(s - jnp.max(s, axis=1, keepdims=True))
        p = e / jnp.sum(e, axis=1, keepdims=True)
        outs.append(_dot(p.astype(BF16), v_ref[:, sl]))
    o_ref[0] = jnp.concatenate(outs, 1).astype(BF16)


def _mem_attention(qc, mem, w_mem_k, w_mem_v, bsz, seq, tq=512):
    n_mem, dm = mem.shape[1:]
    cw = C_HEADS * C_HEAD_DIM
    fixed = lambda b, j: (0, 0)
    return pl.pallas_call(
        _memattn_kernel,
        grid=(bsz, seq // tq),
        in_specs=[pl.BlockSpec((1, tq, cw), lambda b, j: (b, j, 0)),
                  pl.BlockSpec((1, n_mem, dm), lambda b, j: (b, 0, 0)),
                  pl.BlockSpec((dm, cw), fixed), pl.BlockSpec((dm, cw), fixed)],
        out_specs=pl.BlockSpec((1, tq, cw), lambda b, j: (b, j, 0)),
        out_shape=jax.ShapeDtypeStruct((bsz, seq, cw), BF16),
        scratch_shapes=[pltpu.VMEM((n_mem, cw), BF16), pltpu.VMEM((n_mem, cw), BF16)],
        compiler_params=_cparams(("arbitrary", "arbitrary")),
        name="memattn",
    )(qc.reshape(bsz, seq, cw), mem, w_mem_k.astype(BF16), w_mem_v.astype(BF16))


def _mix_kernel(h0_ref, oa_ref, u2_ref, oc_ref, wg_ref, bg_ref, woa_ref, wpw_ref, bpw_ref, woc_ref,
                wout_ref, bout_ref, g_ref, b_ref, wr_ref, br_ref, h1_ref, rl_ref):
    h0 = h0_ref[...]
    hb = h0.astype(BF16)
    dm = h0.shape[1]
    y = (_dot(oa_ref[...], woa_ref[...]),
         _dot(u2_ref[...], wpw_ref[...]) + bpw_ref[...],
         _dot(oc_ref[...], woc_ref[...]))
    mixed = jnp.zeros_like(h0)
    for k in range(N_BRANCH):
        gate = jax.nn.sigmoid(_dot(hb, wg_ref[:, k * dm:(k + 1) * dm]) + bg_ref[:, k * dm:(k + 1) * dm])
        mixed = mixed + gate * y[k]
    out = _dot(mixed.astype(BF16), wout_ref[...]) + bout_ref[...]
    h1 = _layer_norm(DN_ALPHA * h0 + out, g_ref[...], b_ref[...])
    h1_ref[...] = h1
    rl_ref[...] = _dot(h1.astype(BF16), wr_ref[...]) + br_ref[...]


def _mix(h0, oa, u2, oc, w_gate, b_gate, w_o_a, w_pw2, b_pw2, w_o_c, w_out, b_out, ln_g, ln_b, w_router, b_router,
         tm=256):
    n, dm = h0.shape
    row = lambda i: (i, 0)
    fixed = lambda i: (0, 0)
    wr = jnp.zeros((dm, LANES), F32).at[:, :N_EXPERTS].set(w_router).astype(BF16)
    br = jnp.zeros((1, LANES), F32).at[0, :N_EXPERTS].set(b_router)
    full = lambda a: pl.BlockSpec(a.shape, fixed)
    ws = [w_gate.astype(BF16), b_gate.reshape(1, -1), w_o_a.astype(BF16), w_pw2.astype(BF16), b_pw2.reshape(1, -1),
          w_o_c.astype(BF16), w_out.astype(BF16), b_out.reshape(1, -1), ln_g.reshape(1, -1), ln_b.reshape(1, -1),
          wr, br]
    return pl.pallas_call(
        _mix_kernel,
        grid=(n // tm,),
        in_specs=[pl.BlockSpec((tm, dm), row), pl.BlockSpec((tm, 512), row), pl.BlockSpec((tm, 512), row),
                  pl.BlockSpec((tm, 512), row)] + [full(a) for a in ws],
        out_specs=[pl.BlockSpec((tm, dm), row), pl.BlockSpec((tm, LANES), row)],
        out_shape=[jax.ShapeDtypeStruct((n, dm), F32), jax.ShapeDtypeStruct((n, LANES), F32)],
        compiler_params=_cparams(("arbitrary",)),
        name="mix",
    )(h0, oa, u2, oc, *ws)


def _route_kernel(rl_ref, eid_ref, gate_ref, rank_ref, cnt_ref, carry_ref):
    @pl.when(pl.program_id(0) == 0)
    def _():
        carry_ref[...] = jnp.zeros_like(carry_ref)

    tm = rl_ref.shape[0]
    lane = lax.broadcasted_iota(I32, (tm, LANES), 1)
    vals = jnp.where(lane < N_EXPERTS, rl_ref[...], -jnp.inf)
    top_v, top_i, sels = [], [], []
    for _ in range(TOP_K):
        m = jnp.max(vals, axis=1, keepdims=True)
        idx = jnp.min(jnp.where(vals == m, lane, LANES), axis=1, keepdims=True)
        sel = lane == idx
        top_v.append(m)
        top_i.append(idx)
        sels.append(sel)
        vals = jnp.where(sel, -jnp.inf, vals)
    es = [jnp.exp(v - top_v[0]) for v in top_v]
    denom = es[0] + es[1] + es[2] + es[3]
    onehot = jnp.zeros((tm, LANES), F32)
    for sel in sels:
        onehot = onehot + jnp.where(sel, 1.0, 0.0)
    r = lax.broadcasted_iota(I32, (tm, tm), 0)
    c = lax.broadcasted_iota(I32, (tm, tm), 1)
    tri = jnp.where(c < r, 1.0, 0.0).astype(BF16)
    before = _dot(tri, onehot.astype(BF16)) + carry_ref[...]
    carry = carry_ref[...] + jnp.sum(onehot, axis=0, keepdims=True)
    carry_ref[...] = carry
    cnt_ref[...] = carry
    eid = jnp.zeros((tm, LANES), I32)
    gate = jnp.zeros((tm, LANES), F32)
    rank = jnp.zeros((tm, LANES), F32)
    for k in range(TOP_K):
        eid = jnp.where(lane == k, top_i[k], eid)
        gate = jnp.where(lane == k, es[k] / denom, gate)
        rank = jnp.where(lane == k, jnp.sum(jnp.where(sels[k], before, 0.0), axis=1, keepdims=True), rank)
    eid_ref[...] = eid
    gate_ref[...] = gate
    rank_ref[...] = rank.astype(I32)


def _route(rl, tm=512):
    n = rl.shape[0]
    row = lambda i: (i, 0)
    return pl.pallas_call(
        _route_kernel,
        grid=(n // tm,),
        in_specs=[pl.BlockSpec((tm, LANES), row)],
        out_specs=[pl.BlockSpec((tm, LANES), row), pl.BlockSpec((tm, LANES), row), pl.BlockSpec((tm, LANES), row),
                   pl.BlockSpec((1, LANES), lambda i: (0, 0))],
        out_shape=[jax.ShapeDtypeStruct((n, LANES), I32), jax.ShapeDtypeStruct((n, LANES), F32),
                   jax.ShapeDtypeStruct((n, LANES), I32), jax.ShapeDtypeStruct((1, LANES), F32)],
        scratch_shapes=[pltpu.VMEM((1, LANES), F32)],
        compiler_params=_cparams(("arbitrary",)),
        name="route",
    )(rl)


SUBLANES = 8
DMA_UNROLL = 8


def _to_tiles(x, tile_ref):
    for s in range(SUBLANES):
        tile_ref[:, s, :] = x[:, s * LANES:(s + 1) * LANES]


def _from_tiles(tile_ref):
    return jnp.concatenate([tile_ref[:, s, :] for s in range(SUBLANES)], axis=1)


def _dispatch_kernel(dest_ref, h1_ref, xs_in_hbm, xs_hbm, stage_ref, sem):
    del xs_in_hbm
    n_copies = dest_ref.shape[2]
    _to_tiles(h1_ref[...], stage_ref)

    def start(r, c):
        pltpu.make_async_copy(stage_ref.at[r // TOP_K], xs_hbm.at[dest_ref[0, 0, r]], sem).start()
        return c

    def wait(r, c):
        pltpu.make_async_copy(stage_ref.at[0], xs_hbm.at[0], sem).wait()
        return c

    lax.fori_loop(0, n_copies, start, 0, unroll=DMA_UNROLL)
    lax.fori_loop(0, n_copies, wait, 0, unroll=DMA_UNROLL)


def _dispatch(h1, dest, n_rows, tm=256):
    n, dm = h1.shape
    steps = n // tm
    dest3 = dest.reshape(steps, 1, tm * TOP_K)
    xs0 = jnp.zeros((n_rows, SUBLANES, LANES), F32)
    return pl.pallas_call(
        _dispatch_kernel,
        grid=(steps,),
        in_specs=[pl.BlockSpec((1, 1, tm * TOP_K), lambda i: (i, 0, 0), memory_space=pltpu.SMEM),
                  pl.BlockSpec((tm, dm), lambda i: (i, 0)), pl.BlockSpec(memory_space=pl.ANY)],
        out_specs=pl.BlockSpec(memory_space=pl.ANY),
        out_shape=jax.ShapeDtypeStruct((n_rows, SUBLANES, LANES), F32),
        scratch_shapes=[pltpu.VMEM((tm, SUBLANES, LANES), F32), pltpu.SemaphoreType.DMA(())],
        input_output_aliases={2: 0},
        compiler_params=_cparams(("arbitrary",)),
        name="dispatch",
    )(dest3, h1, xs0)


def _experts_kernel(blk_exp_ref, n_used_ref, xs_ref, w1_ref, b1_ref, w2_ref, b2_ref, ys_ref, w1b_ref, w2b_ref):
    b = pl.program_id(0)
    prev = blk_exp_ref[jnp.maximum(b - 1, 0)]
    used = b < n_used_ref[0]

    @pl.when(used & ((b == 0) | (blk_exp_ref[b] != prev)))
    def _():
        w1b_ref[...] = w1_ref[0].astype(BF16)
        w2b_ref[...] = w2_ref[0].astype(BF16)

    @pl.when(used)
    def _():
        d_exp = w2b_ref.shape[0]
        hdn = _dot(_from_tiles(xs_ref).astype(BF16), w1b_ref[...]) + b1_ref[0]
        g = jnp.minimum(hdn[:, :d_exp], SWIGLU_LIMIT)
        u = jnp.clip(hdn[:, d_exp:], -SWIGLU_LIMIT, SWIGLU_LIMIT)
        act = (u + 1.0) * (g * jax.nn.sigmoid(SWIGLU_ALPHA * g))
        _to_tiles(_dot(act.astype(BF16), w2b_ref[...]) + b2_ref[0], ys_ref)

    @pl.when(jnp.logical_not(used))
    def _():
        ys_ref[...] = jnp.zeros_like(ys_ref)


def _experts(xs, blk_exp, n_used, w1, b1, w2, b2):
    n_rows = xs.shape[0]
    n_exp, dm, two_f = w1.shape
    d_exp = two_f // 2
    grid_spec = pltpu.PrefetchScalarGridSpec(
        num_scalar_prefetch=2,
        grid=(n_rows // ROW_BLOCK,),
        in_specs=[pl.BlockSpec((ROW_BLOCK, SUBLANES, LANES), lambda b, be, nu: (b, 0, 0)),
                  pl.BlockSpec((1, dm, two_f), lambda b, be, nu: (be[b], 0, 0)),
                  pl.BlockSpec((1, 1, two_f), lambda b, be, nu: (be[b], 0, 0)),
                  pl.BlockSpec((1, d_exp, dm), lambda b, be, nu: (be[b], 0, 0)),
                  pl.BlockSpec((1, 1, dm), lambda b, be, nu: (be[b], 0, 0))],
        out_specs=pl.BlockSpec((ROW_BLOCK, SUBLANES, LANES), lambda b, be, nu: (b, 0, 0)),
        scratch_shapes=[pltpu.VMEM((dm, two_f), BF16), pltpu.VMEM((d_exp, dm), BF16)],
    )
    return pl.pallas_call(
        _experts_kernel,
        grid_spec=grid_spec,
        out_shape=jax.ShapeDtypeStruct((n_rows, SUBLANES, LANES), F32),
        compiler_params=_cparams(("arbitrary",)),
        name="experts",
    )(blk_exp, n_used, xs, w1, b1.reshape(n_exp, 1, two_f), w2, b2.reshape(n_exp, 1, dm))


def _combine_kernel(dest_ref, ys_hbm, h1_ref, gate_ref, g_ref, b_ref, o_ref, buf_ref, sem):
    tm = h1_ref.shape[0]
    n_copies = tm * TOP_K

    def start(r, c):
        pltpu.make_async_copy(ys_hbm.at[dest_ref[0, 0, r]], buf_ref.at[r % TOP_K, r // TOP_K], sem).start()
        return c

    def wait(r, c):
        pltpu.make_async_copy(ys_hbm.at[0], buf_ref.at[0, 0], sem).wait()
        return c

    lax.fori_loop(0, n_copies, start, 0, unroll=DMA_UNROLL)
    lax.fori_loop(0, n_copies, wait, 0, unroll=DMA_UNROLL)
    gate = gate_ref[...]
    moe = jnp.zeros(h1_ref.shape, F32)
    for k in range(TOP_K):
        moe = moe + _from_tiles(buf_ref.at[k]) * gate[:, k:k + 1]
    o_ref[...] = _layer_norm(DN_ALPHA * h1_ref[...] + moe, g_ref[...], b_ref[...])


def _combine(ys, dest, h1, gate, ln_g, ln_b, tm=128):
    n, dm = h1.shape
    steps = n // tm
    dest3 = dest.reshape(steps, 1, tm * TOP_K)
    row = lambda i: (i, 0)
    fixed = lambda i: (0, 0)
    return pl.pallas_call(
        _combine_kernel,
        grid=(steps,),
        in_specs=[pl.BlockSpec((1, 1, tm * TOP_K), lambda i: (i, 0, 0), memory_space=pltpu.SMEM),
                  pl.BlockSpec(memory_space=pl.ANY),
                  pl.BlockSpec((tm, dm), row), pl.BlockSpec((tm, LANES), row),
                  pl.BlockSpec((1, dm), fixed), pl.BlockSpec((1, dm), fixed)],
        out_specs=pl.BlockSpec((tm, dm), row),
        out_shape=jax.ShapeDtypeStruct((n, dm), F32),
        scratch_shapes=[pltpu.VMEM((TOP_K, tm, SUBLANES, LANES), F32), pltpu.SemaphoreType.DMA(())],
        compiler_params=_cparams(("arbitrary",)),
        name="combine",
    )(dest3, ys, h1, gate, ln_g.reshape(1, dm), ln_b.reshape(1, dm))


def _moe(h1, rl, w1, b1, w2, b2, ln_g, ln_b):
    n, dm = h1.shape
    eid, gate, rank, counts = _route(rl)
    eid = eid[:, :TOP_K]
    rank = rank[:, :TOP_K]
    counts = counts[0, :N_EXPERTS].astype(I32)
    padded = (counts + ROW_BLOCK - 1) // ROW_BLOCK * ROW_BLOCK
    pad_end = jnp.cumsum(padded)
    pad_start = pad_end - padded
    dest = (pad_start[eid] + rank).reshape(n * TOP_K)
    n_blocks = n * TOP_K // ROW_BLOCK + N_EXPERTS
    n_used = (pad_end[-1] // ROW_BLOCK).astype(I32)
    blk = jnp.minimum(jnp.arange(n_blocks, dtype=I32), n_used - 1)
    blk_exp = jnp.sum((pad_end[None, :] <= (blk * ROW_BLOCK)[:, None]).astype(I32), axis=1)
    xs = _dispatch(h1, dest, n_blocks * ROW_BLOCK)
    ys = _experts(xs, blk_exp, n_used.reshape(1), w1, b1, w2, b2)
    return _combine(ys, dest, h1, gate, ln_g, ln_b)


def kernel(x, mem, ln0_g, ln0_b, w_in, b_in, kv_norm_g, w_uk, w_uv, w_o_a, w_dw, b_dw, conv_ln_g, conv_ln_b, w_pw2, b_pw2, w_mem_k, w_mem_v, w_o_c, w_out, b_out, ln1_g, ln1_b, w_router, b_router, w1, b1, w2, b2, ln2_g, ln2_b):
    bsz, seq, dm = x.shape
    n = bsz * seq
    assert w_in.shape[0] == DEPTH == 1, "the stem LayerNorm is fused into the single layer's projection"
    l = 0
    gate_lo = w_in.shape[-1] - N_BRANCH * dm
    h0, qa, ckv, qidx, krep, widx, u, qc = _proj(x.reshape(n, dm), ln0_g, ln0_b, w_in[l, :, :gate_lo],
                                                   b_in[l, :gate_lo], kv_norm_g[l])
    oa = _dsa_attention(qidx, widx, qa, krep, ckv, w_uk[l], w_uv[l], bsz, seq).reshape(n, -1)
    u2 = _conv(u, w_dw[l], b_dw[l], conv_ln_g[l], conv_ln_b[l], bsz, seq).reshape(n, -1)
    oc = _mem_attention(qc, mem, w_mem_k[l], w_mem_v[l], bsz, seq).reshape(n, -1)
    h1, rl = _mix(h0, oa, u2, oc, w_in[l, :, gate_lo:], b_in[l, gate_lo:], w_o_a[l], w_pw2[l], b_pw2[l],
                  w_o_c[l], w_out[l], b_out[l], ln1_g[l], ln1_b[l], w_router[l], b_router[l])
    h2 = _moe(h1, rl, w1[l], b1[l], w2[l], b2[l], ln2_g[l], ln2_b[l])
    return h2.reshape(bsz, seq, dm)
```

```python
import functools

import jax
import jax.numpy as jnp
from jax import lax
from jax.experimental import pallas as pl
from jax.experimental.pallas import tpu as pltpu

F32 = jnp.float32
BF16 = jnp.bfloat16
I32 = jnp.int32

CHUNK = 64
A_HEADS = 8
A_HEAD_DIM = 64
KV_RANK = 128
IDX_HEADS = 8
IDX_DIM = 32
TOPK_MAX = 256
CONV_WIDTH = 31
C_HEADS = 4
C_HEAD_DIM = 128
N_BRANCH = 3
N_EXPERTS = 32
TOP_K = 4
SWIGLU_LIMIT = 7.0
SWIGLU_ALPHA = 1.702
LN_EPS = 1e-5
NEG_INF = -1e30
DEPTH = 1
DN_ALPHA = (2 * DEPTH) ** 0.25
A_SCALE = A_HEAD_DIM ** -0.5
IDX_SCALE = (IDX_HEADS * IDX_DIM) ** -0.5
C_SCALE = C_HEAD_DIM ** -0.5

LANES = 128
Q_BLOCK = 128
KEY_TILE = 512
ATTEND_UNROLLS = (4, 2, 1)
ROW_BLOCK = 512
VMEM_LIMIT = 56 * 1024 * 1024


def _cparams(sem, vmem=VMEM_LIMIT):
    return pltpu.CompilerParams(dimension_semantics=sem, vmem_limit_bytes=vmem)


def _layer_norm(x, g, b):
    xc = x - jnp.mean(x, -1, keepdims=True)
    var = jnp.mean(xc * xc, -1, keepdims=True)
    return xc * lax.rsqrt(var + LN_EPS) * g + b


def _dot(a, b):
    return jnp.dot(a, b, preferred_element_type=F32)


def _dot_nt(a, b):
    return lax.dot_general(a, b, (((1,), (1,)), ((), ())), preferred_element_type=F32)


_P_QA = (0, 512)
_P_CKV = (512, 640)
_P_QIDX = (640, 896)
_P_KREP = (896, 1152)
_P_WIDX = (1152, 1280)
_P_GLU_A = (1280, 1792)
_P_GLU_G = (1792, 2304)
_P_QC = (2304, 2816)
_P_COLS = 2816


def _proj_kernel(x_ref, g_ref, b_ref, w_ref, bias_ref, kvg_ref,
                 h0_ref, qa_ref, ckv_ref, qidx_ref, krep_ref, widx_ref, u_ref, qc_ref):
    h = _layer_norm(x_ref[...], g_ref[...], b_ref[...])
    h0_ref[...] = h
    hb = h.astype(BF16)

    def proj(span):
        lo, hi = span
        return _dot(hb, w_ref[:, lo:hi]) + bias_ref[:, lo:hi]

    qa_ref[...] = proj(_P_QA).astype(BF16)
    ckv = proj(_P_CKV)
    ckv = ckv * lax.rsqrt(jnp.mean(ckv * ckv, -1, keepdims=True) + LN_EPS) * kvg_ref[...]
    ckv_ref[...] = ckv.astype(BF16)
    qidx_ref[...] = proj(_P_QIDX).astype(BF16)
    krep_ref[...] = proj(_P_KREP).astype(BF16)
    widx_ref[...] = proj(_P_WIDX) * IDX_SCALE
    u_ref[...] = proj(_P_GLU_A) * jax.nn.sigmoid(proj(_P_GLU_G))
    qc_ref[...] = proj(_P_QC).astype(BF16)


def _proj(xt, ln_g, ln_b, w_in, b_in, kv_g, tm=512):
    n, dm = xt.shape
    o = 0
    parts = {}
    for name, width in (("qa", 512), ("ckv", 128), ("qidx", 256), ("kidx", 32), ("widx", 8),
                        ("glu", 1024), ("qc", 512)):
        parts[name] = (o, o + width)
        o += width

    def cols(a, name):
        lo, hi = parts[name]
        return a[..., lo:hi]

    def pack(a):
        pad = jnp.zeros(a.shape[:-1] + (LANES - 8,), a.dtype)
        glu = cols(a, "glu")
        return jnp.concatenate(
            [cols(a, "qa"), cols(a, "ckv"), cols(a, "qidx"),
             jnp.concatenate([cols(a, "kidx")] * IDX_HEADS, -1),
             cols(a, "widx"), pad, glu[..., :512], glu[..., 512:], cols(a, "qc")], -1)

    w = pack(w_in).astype(BF16)
    bias = pack(b_in).reshape(1, _P_COLS)
    row = lambda i: (i, 0)
    fixed = lambda i: (0, 0)
    outs = [(dm, F32), (512, BF16), (128, BF16), (256, BF16), (256, BF16), (128, F32), (512, F32), (512, BF16)]
    return pl.pallas_call(
        _proj_kernel,
        grid=(n // tm,),
        in_specs=[pl.BlockSpec((tm, dm), row), pl.BlockSpec((1, dm), fixed), pl.BlockSpec((1, dm), fixed),
                  pl.BlockSpec((dm, _P_COLS), fixed), pl.BlockSpec((1, _P_COLS), fixed),
                  pl.BlockSpec((1, KV_RANK), fixed)],
        out_specs=[pl.BlockSpec((tm, c), row) for c, _ in outs],
        out_shape=[jax.ShapeDtypeStruct((n, c), d) for c, d in outs],
        compiler_params=_cparams(("arbitrary",)),
        name="proj",
    )(xt, ln_g.reshape(1, dm), ln_b.reshape(1, dm), w, bias, kv_g.reshape(1, KV_RANK))


def _alibi_slope(h):
    return 2.0 ** (-(8.0 / A_HEADS) * (h + 1))


def _attn_kernel(qidx_ref, widx_ref, qa_ref, krep_ref, ckv_ref, ext_ref, wuk_ref, wuv_ref, o_ref,
                 kp_ref, sc_ref, p_ref, acc_ref, m_ref):
    i = pl.program_id(1)
    t0 = i * Q_BLOCK

    @pl.when(i == 0)
    def _():
        kp_ref[:, :KV_RANK] = ckv_ref[0]
        kp_ref[:, KV_RANK:] = ext_ref[...]

    n_full = i // (KEY_TILE // Q_BLOCK)
    n_tiles = n_full + 1

    def key_rows(ref, j):
        return ref[pl.ds(pl.multiple_of(j * KEY_TILE, KEY_TILE), KEY_TILE), :]

    qi = qidx_ref[0]
    head_of_lane = lax.broadcasted_iota(I32, (1, IDX_HEADS * IDX_DIM), 1) // IDX_DIM
    qbd = jnp.concatenate([jnp.where(head_of_lane == h, qi, jnp.zeros_like(qi)) for h in range(IDX_HEADS)], 0)
    w_t = jnp.transpose(widx_ref[0])

    def score_tile(j):
        s = _dot_nt(key_rows(krep_ref.at[0], j), qbd)
        acc = jnp.zeros((KEY_TILE, Q_BLOCK), F32)
        for h in range(IDX_HEADS):
            acc = acc + jnp.maximum(s[:, h * Q_BLOCK:(h + 1) * Q_BLOCK], 0.0) * w_t[h:h + 1, :]
        return acc + 0.0

    def score_pair(jj, c):
        sc_ref[2 * jj] = score_tile(2 * jj)
        sc_ref[2 * jj + 1] = score_tile(2 * jj + 1)
        return c

    def score_single(j, c):
        sc_ref[j] = score_tile(j)
        return c

    lax.fori_loop(0, n_full // 2, score_pair, 0)
    lax.fori_loop(n_full // 2 * 2, n_full, score_single, 0)
    key_row = lax.broadcasted_iota(I32, (KEY_TILE, Q_BLOCK), 0)
    q_pos = t0 + lax.broadcasted_iota(I32, (KEY_TILE, Q_BLOCK), 1)
    key_last = n_full * KEY_TILE + key_row
    adm_last = (key_last // CHUNK) <= (q_pos // CHUNK)
    sc_ref[n_full] = jnp.where(adm_last, score_tile(n_full), -jnp.inf)
    sc_ref[n_tiles] = jnp.full((KEY_TILE, Q_BLOCK), -jnp.inf, F32)

    def count_where(pred):
        def body(jj, cnt):
            for j in (2 * jj, 2 * jj + 1):
                ind = jnp.where(pred(sc_ref[j], j * KEY_TILE), 1, 0)
                cnt = cnt + jnp.sum(ind.reshape(KEY_TILE // 8, 8, Q_BLOCK), axis=0)
            return cnt
        cnt = lax.fori_loop(0, (n_tiles + 1) // 2, body, jnp.zeros((8, Q_BLOCK), I32))
        return jnp.sum(cnt, axis=0, keepdims=True)

    def key_to_float(key):
        bits = key ^ ((key >> 31) & jnp.int32(0x7FFFFFFF))
        return lax.bitcast_convert_type(bits, F32)

    int_min = jnp.int32(-2 ** 31)

    def bit_step(b, prefix):
        cand = prefix + lax.shift_left(jnp.int32(1), jnp.int32(31) - b)
        cand_f = key_to_float(cand)
        return jnp.where(count_where(lambda s, base: s >= cand_f) >= TOPK_MAX, cand, prefix)

    prefix = lax.fori_loop(0, 32, bit_step, jnp.full((1, Q_BLOCK), int_min, I32))
    thr = jnp.where(prefix == int_min, -jnp.inf, key_to_float(prefix))

    n_gt = count_where(lambda s, base: s > thr)
    n_eq = count_where(lambda s, base: s == thr)
    need = TOPK_MAX - n_gt
    seq_bits = 13

    def tie_search(_):
        def step(b, lo):
            cand = lo + lax.shift_left(jnp.int32(1), jnp.int32(seq_bits) - b)
            below = count_where(lambda s, base: (s == thr) & ((base + key_row) < cand))
            return jnp.where(below < need, cand, lo)
        return lax.fori_loop(0, seq_bits + 1, step, jnp.zeros((1, Q_BLOCK), I32))

    overflow = jnp.max(jnp.where(n_gt + n_eq > TOPK_MAX, 1, 0)) > 0
    last_tie = lax.cond(overflow, tie_search, lambda _: jnp.full((1, Q_BLOCK), 2 ** 30, I32), 0)

    def mask_bias(j):
        s = sc_ref[j]
        tie_ok = jnp.where((j * KEY_TILE + key_row) <= last_tie, 0.0, NEG_INF)
        mb = jnp.where(s > thr, 0.0, jnp.where(s == thr, tie_ok, NEG_INF))
        return jnp.transpose(mb)

    qlat = _dot(qa_ref[0], wuk_ref[...]) * A_SCALE
    lane_e = lax.broadcasted_iota(I32, (Q_BLOCK, LANES), 1)
    t0f = t0.astype(F32)
    qp = []
    for h in range(A_HEADS):
        sl = _alibi_slope(h)
        qext = jnp.where(lane_e == 0, CHUNK * sl, jnp.where(lane_e == 1, sl, jnp.where(lane_e == 2, -sl * t0f, 0.0)))
        qp.append(jnp.concatenate([qlat[:, h * KV_RANK:(h + 1) * KV_RANK], qext], 1))
    qp = jnp.concatenate(qp, 0).astype(BF16)

    m_ref[...] = jnp.full(m_ref.shape, NEG_INF, F32)
    acc_ref[...] = jnp.zeros(acc_ref.shape, F32)

    def attend(j, slot, last=False):
        kt = key_rows(kp_ref, j)
        logits = _dot_nt(qp, kt)
        mb = mask_bias(j)
        if last:
            row_t = t0 + lax.broadcasted_iota(I32, (Q_BLOCK, KEY_TILE), 0)
            key_s = n_full * KEY_TILE + lax.broadcasted_iota(I32, (Q_BLOCK, KEY_TILE), 1)
            mb = jnp.where((key_s // CHUNK) <= (row_t // CHUNK), mb, NEG_INF)
            ahead = jnp.maximum(key_s - row_t, 0).astype(F32)
        for h in range(A_HEADS):
            l = logits[h * Q_BLOCK:(h + 1) * Q_BLOCK] + mb
            if last:
                l = l - (2.0 * _alibi_slope(h)) * ahead
            m_old = m_ref[h]
            m_new = jnp.maximum(m_old, jnp.broadcast_to(jnp.max(l, axis=1, keepdims=True), (Q_BLOCK, LANES)))
            m_ref[h] = m_new
            alpha = jnp.exp(m_old - m_new)
            p = jnp.exp(l - jnp.concatenate([m_new] * (KEY_TILE // LANES), 1))
            p_ref[slot, h * Q_BLOCK:(h + 1) * Q_BLOCK, :] = p.astype(BF16)
            acc_ref[h] = acc_ref[h] * jnp.concatenate([alpha, alpha], 1)
        pv = _dot(p_ref[slot], kt)
        for h in range(A_HEADS):
            acc_ref[h] = acc_ref[h] + pv[h * Q_BLOCK:(h + 1) * Q_BLOCK]

    def attend_group(width, first):
        def body(jj, c):
            for k in range(width):
                attend(first + width * jj + k, k % 2)
            return c
        return body

    done = 0
    for width in ATTEND_UNROLLS:
        trips = (n_full - done) // width
        lax.fori_loop(0, trips, attend_group(width, done), 0)
        done = done + trips * width
    attend(n_full, 0, last=True)

    outs = []
    for h in range(A_HEADS):
        a = acc_ref[h]
        outs.append(a[:, :KV_RANK] / a[:, KV_RANK + 2:KV_RANK + 3])
    olat = jnp.concatenate(outs, 1).astype(BF16)
    o_ref[0] = _dot(olat, wuv_ref[...]).astype(BF16)


def _dsa_attention(qidx, widx, qa, krep, ckv, w_uk, w_uv, bsz, seq):
    nq = seq // Q_BLOCK
    n_key_tiles = seq // KEY_TILE
    pos = lax.broadcasted_iota(I32, (seq, LANES), 0)
    lane = lax.broadcasted_iota(I32, (seq, LANES), 1)
    ext = jnp.where(lane == 0, pos // CHUNK, jnp.where(lane == 1, pos % CHUNK, jnp.where(lane == 2, 1, 0)))
    ext = ext.astype(BF16)
    eye = jnp.eye(A_HEADS, dtype=F32)
    wuk_bd = jnp.einsum("hg,hrd->hdgr", eye, w_uk).reshape(A_HEADS * A_HEAD_DIM, A_HEADS * KV_RANK)
    wuv_bd = jnp.einsum("hg,hrd->hrgd", eye, w_uv).reshape(A_HEADS * KV_RANK, A_HEADS * A_HEAD_DIM)
    blk = lambda b, i: (b, i, 0)
    whole = lambda b, i: (b, 0, 0)
    fixed = lambda b, i: (0, 0)
    r3 = lambda a: a.reshape(bsz, seq, a.shape[-1])
    return pl.pallas_call(
        _attn_kernel,
        grid=(bsz, nq),
        in_specs=[pl.BlockSpec((1, Q_BLOCK, 256), blk), pl.BlockSpec((1, Q_BLOCK, LANES), blk),
                  pl.BlockSpec((1, Q_BLOCK, 512), blk),
                  pl.BlockSpec((1, seq, 256), whole), pl.BlockSpec((1, seq, KV_RANK), whole),
                  pl.BlockSpec((seq, LANES), fixed),
                  pl.BlockSpec((512, 1024), fixed), pl.BlockSpec((1024, 512), fixed)],
        out_specs=pl.BlockSpec((1, Q_BLOCK, 512), blk),
        out_shape=jax.ShapeDtypeStruct((bsz, seq, 512), BF16),
        scratch_shapes=[pltpu.VMEM((seq, 2 * KV_RANK), BF16),
                        pltpu.VMEM((n_key_tiles + 1, KEY_TILE, Q_BLOCK), F32),
                        pltpu.VMEM((2, A_HEADS * Q_BLOCK, KEY_TILE), BF16),
                        pltpu.VMEM((A_HEADS, Q_BLOCK, 2 * KV_RANK), F32),
                        pltpu.VMEM((A_HEADS, Q_BLOCK, LANES), F32)],
        compiler_params=_cparams(("arbitrary", "arbitrary")),
        name="attn",
    )(r3(qidx), r3(widx), r3(qa), r3(krep), r3(ckv), ext, wuk_bd.astype(BF16), wuv_bd.astype(BF16))


CONV_HALO = 32
CONV_PHASES = 8


def _conv_kernel(cur_ref, prev_ref, w_ref, b_ref, g_ref, beta_ref, o_ref, win_ref, shift_ref):
    j = pl.program_id(1)
    tq = cur_ref.shape[1]
    halo = prev_ref[0, tq - CONV_HALO:, :]
    win_ref[:CONV_HALO] = jnp.where(j > 0, halo, jnp.zeros_like(halo))
    win_ref[CONV_HALO:] = cur_ref[0]
    acc = jnp.zeros((tq, cur_ref.shape[2]), F32)
    first = CONV_HALO - (CONV_WIDTH - 1)
    for r in range(CONV_PHASES):
        taps = range(r, CONV_WIDTH, CONV_PHASES)
        rows = taps[-1] - r + tq
        shift_ref[r, :rows] = win_ref[first + r:first + r + rows, :]
        for k in taps:
            acc = acc + shift_ref[r, k - r:k - r + tq] * w_ref[k:k + 1, :]
    y = _layer_norm(acc + b_ref[...], g_ref[...], beta_ref[...])
    o_ref[0] = (y * jax.nn.sigmoid(y)).astype(BF16)


def _conv(u, w_dw, b_dw, ln_g, ln_b, bsz, seq, tq=512):
    c = u.shape[-1]
    u3 = u.reshape(bsz, seq, c)
    fixed = lambda b, j: (0, 0)
    return pl.pallas_call(
        _conv_kernel,
        grid=(bsz, seq // tq),
        in_specs=[pl.BlockSpec((1, tq, c), lambda b, j: (b, j, 0)),
                  pl.BlockSpec((1, tq, c), lambda b, j: (b, jnp.maximum(j - 1, 0), 0)),
                  pl.BlockSpec((CONV_WIDTH, c), fixed), pl.BlockSpec((1, c), fixed),
                  pl.BlockSpec((1, c), fixed), pl.BlockSpec((1, c), fixed)],
        out_specs=pl.BlockSpec((1, tq, c), lambda b, j: (b, j, 0)),
        out_shape=jax.ShapeDtypeStruct((bsz, seq, c), BF16),
        scratch_shapes=[pltpu.VMEM((tq + CONV_HALO, c), F32),
                        pltpu.VMEM((CONV_PHASES, tq + CONV_HALO, c), F32)],
        compiler_params=_cparams(("arbitrary", "arbitrary")),
        name="conv",
    )(u3, u3, w_dw, b_dw.reshape(1, c), ln_g.reshape(1, c), ln_b.reshape(1, c))


def _memattn_kernel(q_ref, mem_ref, wk_ref, wv_ref, o_ref, k_ref, v_ref):
    @pl.when(pl.program_id(1) == 0)
    def _():
        mb = mem_ref[0].astype(BF16)
        k_ref[...] = _dot(mb, wk_ref[...]).astype(BF16)
        v_ref[...] = _dot(mb, wv_ref[...]).astype(BF16)

    q = q_ref[0]
    outs = []
    for h in range(C_HEADS):
        sl = slice(h * C_HEAD_DIM, (h + 1) * C_HEAD_DIM)
        s = _dot_nt(q[:, sl], k_ref[:, sl]) * C_SCALE
        e = jnp.exp(s - jnp.max(s, axis=1, keepdims=True))
        p = e / jnp.sum(e, axis=1, keepdims=True)
        outs.append(_dot(p.astype(BF16), v_ref[:, sl]))
    o_ref[0] = jnp.concatenate(outs, 1).astype(BF16)


def _mem_attention(qc, mem, w_mem_k, w_mem_v, bsz, seq, tq=512):
    n_mem, dm = mem.shape[1:]
    cw = C_HEADS * C_HEAD_DIM
    fixed = lambda b, j: (0, 0)
    return pl.pallas_call(
        _memattn_kernel,
        grid=(bsz, seq // tq),
        in_specs=[pl.BlockSpec((1, tq, cw), lambda b, j: (b, j, 0)),
                  pl.BlockSpec((1, n_mem, dm), lambda b, j: (b, 0, 0)),
                  pl.BlockSpec((dm, cw), fixed), pl.BlockSpec((dm, cw), fixed)],
        out_specs=pl.BlockSpec((1, tq, cw), lambda b, j: (b, j, 0)),
        out_shape=jax.ShapeDtypeStruct((bsz, seq, cw), BF16),
        scratch_shapes=[pltpu.VMEM((n_mem, cw), BF16), pltpu.VMEM((n_mem, cw), BF16)],
        compiler_params=_cparams(("arbitrary", "arbitrary")),
        name="memattn",
    )(qc.reshape(bsz, seq, cw), mem, w_mem_k.astype(BF16), w_mem_v.astype(BF16))


def _mix_kernel(h0_ref, oa_ref, u2_ref, oc_ref, wg_ref, bg_ref, woa_ref, wpw_ref, bpw_ref, woc_ref,
                wout_ref, bout_ref, g_ref, b_ref, wr_ref, br_ref, h1_ref, rl_ref):
    h0 = h0_ref[...]
    hb = h0.astype(BF16)
    dm = h0.shape[1]
    y = (_dot(oa_ref[...], woa_ref[...]),
         _dot(u2_ref[...], wpw_ref[...]) + bpw_ref[...],
         _dot(oc_ref[...], woc_ref[...]))
    mixed = jnp.zeros_like(h0)
    for k in range(N_BRANCH):
        gate = jax.nn.sigmoid(_dot(hb, wg_ref[:, k * dm:(k + 1) * dm]) + bg_ref[:, k * dm:(k + 1) * dm])
        mixed = mixed + gate * y[k]
    out = _dot(mixed.astype(BF16), wout_ref[...]) + bout_ref[...]
    h1 = _layer_norm(DN_ALPHA * h0 + out, g_ref[...], b_ref[...])
    h1_ref[...] = h1
    rl_ref[...] = _dot(h1.astype(BF16), wr_ref[...]) + br_ref[...]


def _mix(h0, oa, u2, oc, w_gate, b_gate, w_o_a, w_pw2, b_pw2, w_o_c, w_out, b_out, ln_g, ln_b, w_router, b_router,
         tm=256):
    n, dm = h0.shape
    row = lambda i: (i, 0)
    fixed = lambda i: (0, 0)
    wr = jnp.zeros((dm, LANES), F32).at[:, :N_EXPERTS].set(w_router).astype(BF16)
    br = jnp.zeros((1, LANES), F32).at[0, :N_EXPERTS].set(b_router)
    full = lambda a: pl.BlockSpec(a.shape, fixed)
    ws = [w_gate.astype(BF16), b_gate.reshape(1, -1), w_o_a.astype(BF16), w_pw2.astype(BF16), b_pw2.reshape(1, -1),
          w_o_c.astype(BF16), w_out.astype(BF16), b_out.reshape(1, -1), ln_g.reshape(1, -1), ln_b.reshape(1, -1),
          wr, br]
    return pl.pallas_call(
        _mix_kernel,
        grid=(n // tm,),
        in_specs=[pl.BlockSpec((tm, dm), row), pl.BlockSpec((tm, 512), row), pl.BlockSpec((tm, 512), row),
                  pl.BlockSpec((tm, 512), row)] + [full(a) for a in ws],
        out_specs=[pl.BlockSpec((tm, dm), row), pl.BlockSpec((tm, LANES), row)],
        out_shape=[jax.ShapeDtypeStruct((n, dm), F32), jax.ShapeDtypeStruct((n, LANES), F32)],
        compiler_params=_cparams(("arbitrary",)),
        name="mix",
    )(h0, oa, u2, oc, *ws)


def _route_kernel(rl_ref, eid_ref, gate_ref, rank_ref, cnt_ref, carry_ref):
    @pl.when(pl.program_id(0) == 0)
    def _():
        carry_ref[...] = jnp.zeros_like(carry_ref)

    tm = rl_ref.shape[0]
    lane = lax.broadcasted_iota(I32, (tm, LANES), 1)
    vals = jnp.where(lane < N_EXPERTS, rl_ref[...], -jnp.inf)
    top_v, top_i, sels = [], [], []
    for _ in range(TOP_K):
        m = jnp.max(vals, axis=1, keepdims=True)
        idx = jnp.min(jnp.where(vals == m, lane, LANES), axis=1, keepdims=True)
        sel = lane == idx
        top_v.append(m)
        top_i.append(idx)
        sels.append(sel)
        vals = jnp.where(sel, -jnp.inf, vals)
    es = [jnp.exp(v - top_v[0]) for v in top_v]
    denom = es[0] + es[1] + es[2] + es[3]
    onehot = jnp.zeros((tm, LANES), F32)
    for sel in sels:
        onehot = onehot + jnp.where(sel, 1.0, 0.0)
    r = lax.broadcasted_iota(I32, (tm, tm), 0)
    c = lax.broadcasted_iota(I32, (tm, tm), 1)
    tri = jnp.where(c < r, 1.0, 0.0).astype(BF16)
    before = _dot(tri, onehot.astype(BF16)) + carry_ref[...]
    carry = carry_ref[...] + jnp.sum(onehot, axis=0, keepdims=True)
    carry_ref[...] = carry
    cnt_ref[...] = carry
    eid = jnp.zeros((tm, LANES), I32)
    gate = jnp.zeros((tm, LANES), F32)
    rank = jnp.zeros((tm, LANES), F32)
    for k in range(TOP_K):
        eid = jnp.where(lane == k, top_i[k], eid)
        gate = jnp.where(lane == k, es[k] / denom, gate)
        rank = jnp.where(lane == k, jnp.sum(jnp.where(sels[k], before, 0.0), axis=1, keepdims=True), rank)
    eid_ref[...] = eid
    gate_ref[...] = gate
    rank_ref[...] = rank.astype(I32)


def _route(rl, tm=512):
    n = rl.shape[0]
    row = lambda i: (i, 0)
    return pl.pallas_call(
        _route_kernel,
        grid=(n // tm,),
        in_specs=[pl.BlockSpec((tm, LANES), row)],
        out_specs=[pl.BlockSpec((tm, LANES), row), pl.BlockSpec((tm, LANES), row), pl.BlockSpec((tm, LANES), row),
                   pl.BlockSpec((1, LANES), lambda i: (0, 0))],
        out_shape=[jax.ShapeDtypeStruct((n, LANES), I32), jax.ShapeDtypeStruct((n, LANES), F32),
                   jax.ShapeDtypeStruct((n, LANES), I32), jax.ShapeDtypeStruct((1, LANES), F32)],
        scratch_shapes=[pltpu.VMEM((1, LANES), F32)],
        compiler_params=_cparams(("arbitrary",)),
        name="route",
    )(rl)


SUBLANES = 8
DMA_UNROLL = 2


def _to_tiles(x, tile_ref):
    for s in range(SUBLANES):
        tile_ref[pl.ds(s, x.shape[0], stride=SUBLANES), :] = x[:, s * LANES:(s + 1) * LANES]


def _from_tiles(tile_ref):
    rows = tile_ref.shape[0] // SUBLANES
    return jnp.concatenate([tile_ref[pl.ds(s, rows, stride=SUBLANES), :] for s in range(SUBLANES)], axis=1)


def _tile_rows(ref, row):
    return ref.at[pl.ds(pl.multiple_of(row * SUBLANES, SUBLANES), SUBLANES)]


def _dispatch_kernel(dest_ref, h1_ref, xs_in_hbm, xs_hbm, stage_ref, sem):
    del xs_in_hbm
    tm = h1_ref.shape[0]
    _to_tiles(h1_ref[...], stage_ref)

    def start(t, c):
        for k in range(TOP_K):
            pltpu.make_async_copy(_tile_rows(stage_ref, t), _tile_rows(xs_hbm, dest_ref[0, 0, TOP_K * t + k]),
                                  sem).start(priority=k % 2)
        return c

    def wait(t, c):
        for k in range(TOP_K):
            pltpu.make_async_copy(_tile_rows(stage_ref, 0), _tile_rows(xs_hbm, 0), sem).wait()
        return c

    lax.fori_loop(0, tm, start, 0, unroll=DMA_UNROLL)
    lax.fori_loop(0, tm, wait, 0, unroll=DMA_UNROLL)


def _dispatch(h1, dest, n_rows, tm=256):
    n, dm = h1.shape
    steps = n // tm
    dest3 = dest.reshape(steps, 1, tm * TOP_K)
    xs0 = jnp.zeros((n_rows * SUBLANES, LANES), F32)
    return pl.pallas_call(
        _dispatch_kernel,
        grid=(steps,),
        in_specs=[pl.BlockSpec((1, 1, tm * TOP_K), lambda i: (i, 0, 0), memory_space=pltpu.SMEM),
                  pl.BlockSpec((tm, dm), lambda i: (i, 0)), pl.BlockSpec(memory_space=pl.ANY)],
        out_specs=pl.BlockSpec(memory_space=pl.ANY),
        out_shape=jax.ShapeDtypeStruct((n_rows * SUBLANES, LANES), F32),
        scratch_shapes=[pltpu.VMEM((tm * SUBLANES, LANES), F32), pltpu.SemaphoreType.DMA(())],
        input_output_aliases={2: 0},
        compiler_params=_cparams(("arbitrary",)),
        name="dispatch",
    )(dest3, h1, xs0)


def _experts_kernel(blk_exp_ref, n_used_ref, xs_ref, w1_ref, b1_ref, w2_ref, b2_ref, ys_ref, w1b_ref, w2b_ref):
    b = pl.program_id(0)
    prev = blk_exp_ref[jnp.maximum(b - 1, 0)]
    used = b < n_used_ref[0]

    @pl.when(used & ((b == 0) | (blk_exp_ref[b] != prev)))
    def _():
        w1b_ref[...] = w1_ref[0].astype(BF16)
        w2b_ref[...] = w2_ref[0].astype(BF16)

    @pl.when(used)
    def _():
        d_exp = w2b_ref.shape[0]
        hdn = _dot(_from_tiles(xs_ref).astype(BF16), w1b_ref[...]) + b1_ref[0]
        g = jnp.minimum(hdn[:, :d_exp], SWIGLU_LIMIT)
        u = jnp.clip(hdn[:, d_exp:], -SWIGLU_LIMIT, SWIGLU_LIMIT)
        act = (u + 1.0) * (g * jax.nn.sigmoid(SWIGLU_ALPHA * g))
        _to_tiles(_dot(act.astype(BF16), w2b_ref[...]) + b2_ref[0], ys_ref)

    @pl.when(jnp.logical_not(used))
    def _():
        ys_ref[...] = jnp.zeros_like(ys_ref)


def _experts(xs, blk_exp, n_used, w1, b1, w2, b2):
    n_rows = xs.shape[0] // SUBLANES
    n_exp, dm, two_f = w1.shape
    d_exp = two_f // 2
    grid_spec = pltpu.PrefetchScalarGridSpec(
        num_scalar_prefetch=2,
        grid=(n_rows // ROW_BLOCK,),
        in_specs=[pl.BlockSpec((ROW_BLOCK * SUBLANES, LANES), lambda b, be, nu: (b, 0)),
                  pl.BlockSpec((1, dm, two_f), lambda b, be, nu: (be[b], 0, 0)),
                  pl.BlockSpec((1, 1, two_f), lambda b, be, nu: (be[b], 0, 0)),
                  pl.BlockSpec((1, d_exp, dm), lambda b, be, nu: (be[b], 0, 0)),
                  pl.BlockSpec((1, 1, dm), lambda b, be, nu: (be[b], 0, 0))],
        out_specs=pl.BlockSpec((ROW_BLOCK * SUBLANES, LANES), lambda b, be, nu: (b, 0)),
        scratch_shapes=[pltpu.VMEM((dm, two_f), BF16), pltpu.VMEM((d_exp, dm), BF16)],
    )
    return pl.pallas_call(
        _experts_kernel,
        grid_spec=grid_spec,
        out_shape=jax.ShapeDtypeStruct((n_rows * SUBLANES, LANES), F32),
        compiler_params=_cparams(("arbitrary",)),
        name="experts",
    )(blk_exp, n_used, xs, w1, b1.reshape(n_exp, 1, two_f), w2, b2.reshape(n_exp, 1, dm))


def _combine_kernel(dest_ref, ys_hbm, h1_ref, gate_ref, g_ref, b_ref, o_ref, buf_ref, sem):
    tm = h1_ref.shape[0]

    def start(t, c):
        for k in range(TOP_K):
            pltpu.make_async_copy(_tile_rows(ys_hbm, dest_ref[0, 0, TOP_K * t + k]), _tile_rows(buf_ref.at[k], t),
                                  sem).start(priority=k % 2)
        return c

    def wait(t, c):
        for k in range(TOP_K):
            pltpu.make_async_copy(_tile_rows(ys_hbm, 0), _tile_rows(buf_ref.at[0], 0), sem).wait()
        return c

    lax.fori_loop(0, tm, start, 0, unroll=DMA_UNROLL)
    lax.fori_loop(0, tm, wait, 0, unroll=DMA_UNROLL)
    gate = gate_ref[...]
    moe = jnp.zeros(h1_ref.shape, F32)
    for k in range(TOP_K):
        moe = moe + _from_tiles(buf_ref.at[k]) * gate[:, k:k + 1]
    o_ref[...] = _layer_norm(DN_ALPHA * h1_ref[...] + moe, g_ref[...], b_ref[...])


def _combine(ys, dest, h1, gate, ln_g, ln_b, tm=128):
    n, dm = h1.shape
    steps = n // tm
    dest3 = dest.reshape(steps, 1, tm * TOP_K)
    row = lambda i: (i, 0)
    fixed = lambda i: (0, 0)
    return pl.pallas_call(
        _combine_kernel,
        grid=(steps,),
        in_specs=[pl.BlockSpec((1, 1, tm * TOP_K), lambda i: (i, 0, 0), memory_space=pltpu.SMEM),
                  pl.BlockSpec(memory_space=pl.ANY),
                  pl.BlockSpec((tm, dm), row), pl.BlockSpec((tm, LANES), row),
                  pl.BlockSpec((1, dm), fixed), pl.BlockSpec((1, dm), fixed)],
        out_specs=pl.BlockSpec((tm, dm), row),
        out_shape=jax.ShapeDtypeStruct((n, dm), F32),
        scratch_shapes=[pltpu.VMEM((TOP_K, tm * SUBLANES, LANES), F32), pltpu.SemaphoreType.DMA(())],
        compiler_params=_cparams(("arbitrary",)),
        name="combine",
    )(dest3, ys, h1, gate, ln_g.reshape(1, dm), ln_b.reshape(1, dm))


def _moe(h1, rl, w1, b1, w2, b2, ln_g, ln_b):
    n, dm = h1.shape
    eid, gate, rank, counts = _route(rl)
    eid = eid[:, :TOP_K]
    rank = rank[:, :TOP_K]
    counts = counts[0, :N_EXPERTS].astype(I32)
    padded = (counts + ROW_BLOCK - 1) // ROW_BLOCK * ROW_BLOCK
    pad_end = jnp.cumsum(padded)
    pad_start = pad_end - padded
    dest = (pad_start[eid] + rank).reshape(n * TOP_K)
    n_blocks = n * TOP_K // ROW_BLOCK + N_EXPERTS
    n_used = (pad_end[-1] // ROW_BLOCK).astype(I32)
    blk = jnp.minimum(jnp.arange(n_blocks, dtype=I32), n_used - 1)
    blk_exp = jnp.sum((pad_end[None, :] <= (blk * ROW_BLOCK)[:, None]).astype(I32), axis=1)
    xs = _dispatch(h1, dest, n_blocks * ROW_BLOCK)
    ys = _experts(xs, blk_exp, n_used.reshape(1), w1, b1, w2, b2)
    return _combine(ys, dest, h1, gate, ln_g, ln_b)


def kernel(x, mem, ln0_g, ln0_b, w_in, b_in, kv_norm_g, w_uk, w_uv, w_o_a, w_dw, b_dw, conv_ln_g, conv_ln_b, w_pw2, b_pw2, w_mem_k, w_mem_v, w_o_c, w_out, b_out, ln1_g, ln1_b, w_router, b_router, w1, b1, w2, b2, ln2_g, ln2_b):
    bsz, seq, dm = x.shape
    n = bsz * seq
    assert w_in.shape[0] == DEPTH == 1, "the stem LayerNorm is fused into the single layer's projection"
    l = 0
    gate_lo = w_in.shape[-1] - N_BRANCH * dm
    h0, qa, ckv, qidx, krep, widx, u, qc = _proj(x.reshape(n, dm), ln0_g, ln0_b, w_in[l, :, :gate_lo],
                                                   b_in[l, :gate_lo], kv_norm_g[l])
    oa = _dsa_attention(qidx, widx, qa, krep, ckv, w_uk[l], w_uv[l], bsz, seq).reshape(n, -1)
    u2 = _conv(u, w_dw[l], b_dw[l], conv_ln_g[l], conv_ln_b[l], bsz, seq).reshape(n, -1)
    oc = _mem_attention(qc, mem, w_mem_k[l], w_mem_v[l], bsz, seq).reshape(n, -1)
    h1, rl = _mix(h0, oa, u2, oc, w_in[l, :, gate_lo:], b_in[l, gate_lo:], w_o_a[l], w_pw2[l], b_pw2[l],
                  w_o_c[l], w_out[l], b_out[l], ln1_g[l], ln1_b[l], w_router[l], b_router[l])
    h2 = _moe(h1, rl, w1[l], b1[l], w2[l], b2[l], ln2_g[l], ln2_b[l])
    return h2.reshape(bsz, seq, dm)
```

```python
import functools

import jax
import jax.numpy as jnp
from jax import lax
from jax.experimental import pallas as pl
from jax.experimental.pallas import tpu as pltpu

F32 = jnp.float32
BF16 = jnp.bfloat16
I32 = jnp.int32

CHUNK = 64
A_HEADS = 8
A_HEAD_DIM = 64
KV_RANK = 128
IDX_HEADS = 8
IDX_DIM = 32
TOPK_MAX = 256
CONV_WIDTH = 31
C_HEADS = 4
C_HEAD_DIM = 128
N_BRANCH = 3
N_EXPERTS = 32
TOP_K = 4
SWIGLU_LIMIT = 7.0
SWIGLU_ALPHA = 1.702
LN_EPS = 1e-5
NEG_INF = -1e30
DEPTH = 1
DN_ALPHA = (2 * DEPTH) ** 0.25
A_SCALE = A_HEAD_DIM ** -0.5
IDX_SCALE = (IDX_HEADS * IDX_DIM) ** -0.5
C_SCALE = C_HEAD_DIM ** -0.5

LANES = 128
Q_BLOCK = 128
KEY_TILE = 512
ATTEND_UNROLLS = (4, 2, 1)
COARSE_BITS = 20
FINE_KEEP = 4
FINE_CHAINS = 4
ROW_BLOCK = 512
VMEM_LIMIT = 56 * 1024 * 1024


def _cparams(sem, vmem=VMEM_LIMIT):
    return pltpu.CompilerParams(dimension_semantics=sem, vmem_limit_bytes=vmem)


def _layer_norm(x, g, b):
    xc = x - jnp.mean(x, -1, keepdims=True)
    var = jnp.mean(xc * xc, -1, keepdims=True)
    return xc * lax.rsqrt(var + LN_EPS) * g + b


def _dot(a, b):
    return jnp.dot(a, b, preferred_element_type=F32)


def _dot_nt(a, b):
    return lax.dot_general(a, b, (((1,), (1,)), ((), ())), preferred_element_type=F32)


_P_QA = (0, 512)
_P_CKV = (512, 640)
_P_QIDX = (640, 896)
_P_KREP = (896, 1152)
_P_WIDX = (1152, 1280)
_P_GLU_A = (1280, 1792)
_P_GLU_G = (1792, 2304)
_P_QC = (2304, 2816)
_P_COLS = 2816


def _proj_kernel(x_ref, g_ref, b_ref, w_ref, bias_ref, kvg_ref,
                 h0_ref, qa_ref, ckv_ref, qidx_ref, krep_ref, widx_ref, u_ref, qc_ref):
    h = _layer_norm(x_ref[...], g_ref[...], b_ref[...])
    h0_ref[...] = h
    hb = h.astype(BF16)

    def proj(span):
        lo, hi = span
        return _dot(hb, w_ref[:, lo:hi]) + bias_ref[:, lo:hi]

    qa_ref[...] = proj(_P_QA).astype(BF16)
    ckv = proj(_P_CKV)
    ckv = ckv * lax.rsqrt(jnp.mean(ckv * ckv, -1, keepdims=True) + LN_EPS) * kvg_ref[...]
    ckv_ref[...] = ckv.astype(BF16)
    qidx_ref[...] = proj(_P_QIDX).astype(BF16)
    krep_ref[...] = proj(_P_KREP).astype(BF16)
    widx_ref[...] = proj(_P_WIDX) * IDX_SCALE
    u_ref[...] = proj(_P_GLU_A) * jax.nn.sigmoid(proj(_P_GLU_G))
    qc_ref[...] = proj(_P_QC).astype(BF16)


def _proj(xt, ln_g, ln_b, w_in, b_in, kv_g, tm=512):
    n, dm = xt.shape
    o = 0
    parts = {}
    for name, width in (("qa", 512), ("ckv", 128), ("qidx", 256), ("kidx", 32), ("widx", 8),
                        ("glu", 1024), ("qc", 512)):
        parts[name] = (o, o + width)
        o += width

    def cols(a, name):
        lo, hi = parts[name]
        return a[..., lo:hi]

    def pack(a):
        pad = jnp.zeros(a.shape[:-1] + (LANES - 8,), a.dtype)
        glu = cols(a, "glu")
        return jnp.concatenate(
            [cols(a, "qa"), cols(a, "ckv"), cols(a, "qidx"),
             jnp.concatenate([cols(a, "kidx")] * IDX_HEADS, -1),
             cols(a, "widx"), pad, glu[..., :512], glu[..., 512:], cols(a, "qc")], -1)

    w = pack(w_in).astype(BF16)
    bias = pack(b_in).reshape(1, _P_COLS)
    row = lambda i: (i, 0)
    fixed = lambda i: (0, 0)
    outs = [(dm, F32), (512, BF16), (128, BF16), (256, BF16), (256, BF16), (128, F32), (512, F32), (512, BF16)]
    return pl.pallas_call(
        _proj_kernel,
        grid=(n // tm,),
        in_specs=[pl.BlockSpec((tm, dm), row), pl.BlockSpec((1, dm), fixed), pl.BlockSpec((1, dm), fixed),
                  pl.BlockSpec((dm, _P_COLS), fixed), pl.BlockSpec((1, _P_COLS), fixed),
                  pl.BlockSpec((1, KV_RANK), fixed)],
        out_specs=[pl.BlockSpec((tm, c), row) for c, _ in outs],
        out_shape=[jax.ShapeDtypeStruct((n, c), d) for c, d in outs],
        compiler_params=_cparams(("arbitrary",)),
        name="proj",
    )(xt, ln_g.reshape(1, dm), ln_b.reshape(1, dm), w, bias, kv_g.reshape(1, KV_RANK))


def _alibi_slope(h):
    return 2.0 ** (-(8.0 / A_HEADS) * (h + 1))


def _attn_kernel(qidx_ref, widx_ref, qa_ref, krep_ref, ckv_ref, ext_ref, wuk_ref, wuv_ref, o_ref,
                 kp_ref, sc_ref, p_ref, acc_ref, m_ref):
    i = pl.program_id(1)
    t0 = i * Q_BLOCK

    @pl.when(i == 0)
    def _():
        kp_ref[:, :KV_RANK] = ckv_ref[0]
        kp_ref[:, KV_RANK:] = ext_ref[...]

    n_full = i // (KEY_TILE // Q_BLOCK)
    n_tiles = n_full + 1

    def key_rows(ref, j):
        return ref[pl.ds(pl.multiple_of(j * KEY_TILE, KEY_TILE), KEY_TILE), :]

    qi = qidx_ref[0]
    head_of_lane = lax.broadcasted_iota(I32, (1, IDX_HEADS * IDX_DIM), 1) // IDX_DIM
    qbd = jnp.concatenate([jnp.where(head_of_lane == h, qi, jnp.zeros_like(qi)) for h in range(IDX_HEADS)], 0)
    w_t = jnp.transpose(widx_ref[0])

    def score_tile(j):
        s = _dot_nt(key_rows(krep_ref.at[0], j), qbd)
        acc = jnp.zeros((KEY_TILE, Q_BLOCK), F32)
        for h in range(IDX_HEADS):
            acc = acc + jnp.maximum(s[:, h * Q_BLOCK:(h + 1) * Q_BLOCK], 0.0) * w_t[h:h + 1, :]
        return acc + 0.0

    def score_pair(jj, c):
        sc_ref[2 * jj] = score_tile(2 * jj)
        sc_ref[2 * jj + 1] = score_tile(2 * jj + 1)
        return c

    def score_single(j, c):
        sc_ref[j] = score_tile(j)
        return c

    lax.fori_loop(0, n_full // 2, score_pair, 0)
    lax.fori_loop(n_full // 2 * 2, n_full, score_single, 0)
    key_row = lax.broadcasted_iota(I32, (KEY_TILE, Q_BLOCK), 0)
    q_pos = t0 + lax.broadcasted_iota(I32, (KEY_TILE, Q_BLOCK), 1)
    key_last = n_full * KEY_TILE + key_row
    adm_last = (key_last // CHUNK) <= (q_pos // CHUNK)
    sc_ref[n_full] = jnp.where(adm_last, score_tile(n_full), -jnp.inf)
    sc_ref[n_tiles] = jnp.full((KEY_TILE, Q_BLOCK), -jnp.inf, F32)

    def count_where(*preds):
        def body(jj, cnts):
            cnts = list(cnts)
            for j in (2 * jj, 2 * jj + 1):
                s = sc_ref[j]
                for n, pred in enumerate(preds):
                    ind = jnp.where(pred(s, j * KEY_TILE), 1, 0)
                    cnts[n] = cnts[n] + jnp.sum(ind.reshape(KEY_TILE // 8, 8, Q_BLOCK), axis=0)
            return tuple(cnts)
        zero = jnp.zeros((8, Q_BLOCK), I32)
        cnts = lax.fori_loop(0, (n_tiles + 1) // 2, body, (zero,) * len(preds))
        cnts = [jnp.sum(c, axis=0, keepdims=True) for c in cnts]
        return cnts[0] if len(preds) == 1 else cnts

    def key_to_float(key):
        bits = key ^ ((key >> 31) & jnp.int32(0x7FFFFFFF))
        return lax.bitcast_convert_type(bits, F32)

    int_min = jnp.int32(-2 ** 31)

    def bit_step(b, prefix):
        cand = prefix + lax.shift_left(jnp.int32(1), jnp.int32(31) - b)
        cand_f = key_to_float(cand)
        return jnp.where(count_where(lambda s, base: s >= cand_f) >= TOPK_MAX, cand, prefix)

    coarse = lax.fori_loop(0, COARSE_BITS, bit_step, jnp.full((1, Q_BLOCK), int_min, I32))
    enough = coarse != int_min

    top_f = key_to_float(coarse + jnp.int32(1 << (32 - COARSE_BITS)))

    def insert(best, v):
        out = []
        for b in best:
            out.append(jnp.maximum(b, v))
            v = jnp.minimum(b, v)
        return out

    def merge(best, other):
        for v in other:
            best = insert(best, v)
        return best

    def fine_scan(jj, carry):
        cnt, lists = carry[0], [list(carry[1 + FINE_KEEP * c:1 + FINE_KEEP * (c + 1)]) for c in range(FINE_CHAINS)]
        for j in (2 * jj, 2 * jj + 1):
            s = sc_ref[j]
            above = s >= top_f
            cnt = cnt + jnp.sum(jnp.where(above, 1, 0).reshape(KEY_TILE // 8, 8, Q_BLOCK), axis=0)
            below = jnp.where(above, -jnp.inf, s).reshape(KEY_TILE // 8, 8, Q_BLOCK)
            for g in range(KEY_TILE // 8):
                lists[g % FINE_CHAINS] = insert(lists[g % FINE_CHAINS], below[g])
        return (cnt,) + tuple(v for lst in lists for v in lst)

    lowest = jnp.full((8, Q_BLOCK), -jnp.inf, F32)
    carry = lax.fori_loop(0, (n_tiles + 1) // 2, fine_scan,
                          (jnp.zeros((8, Q_BLOCK), I32),) + (lowest,) * (FINE_KEEP * FINE_CHAINS))
    n_above = jnp.sum(carry[0], axis=0, keepdims=True)
    best = list(carry[1:1 + FINE_KEEP])
    for c in range(1, FINE_CHAINS):
        best = merge(best, carry[1 + FINE_KEEP * c:1 + FINE_KEEP * (c + 1)])
    top = [b[0:1] for b in best]
    for sub in range(1, 8):
        top = merge(top, [b[sub:sub + 1] for b in best])
    r = TOPK_MAX - n_above
    thr_fast = top[FINE_KEEP - 1]
    for k in range(FINE_KEEP - 1, 0, -1):
        thr_fast = jnp.where(r == k, top[k - 1], thr_fast)
    resolved = jnp.min(jnp.where((r <= FINE_KEEP) | jnp.logical_not(enough), 1, 0)) > 0

    def remaining_bits(_):
        return key_to_float(lax.fori_loop(COARSE_BITS, 32, bit_step, coarse))

    thr = lax.cond(resolved, lambda _: thr_fast, remaining_bits, 0)
    thr = jnp.where(enough, thr, -jnp.inf)

    n_gt, n_eq = count_where(lambda s, base: s > thr, lambda s, base: s == thr)
    need = TOPK_MAX - n_gt
    seq_bits = 13

    def tie_search(_):
        def step(b, lo):
            cand = lo + lax.shift_left(jnp.int32(1), jnp.int32(seq_bits) - b)
            below = count_where(lambda s, base: (s == thr) & ((base + key_row) < cand))
            return jnp.where(below < need, cand, lo)
        return lax.fori_loop(0, seq_bits + 1, step, jnp.zeros((1, Q_BLOCK), I32))

    overflow = jnp.max(jnp.where(n_gt + n_eq > TOPK_MAX, 1, 0)) > 0
    last_tie = lax.cond(overflow, tie_search, lambda _: jnp.full((1, Q_BLOCK), 2 ** 30, I32), 0)

    def mask_bias(j):
        s = sc_ref[j]
        tie_ok = jnp.where((j * KEY_TILE + key_row) <= last_tie, 0.0, NEG_INF)
        mb = jnp.where(s > thr, 0.0, jnp.where(s == thr, tie_ok, NEG_INF))
        return jnp.transpose(mb)

    qlat = _dot(qa_ref[0], wuk_ref[...]) * A_SCALE
    lane_e = lax.broadcasted_iota(I32, (Q_BLOCK, LANES), 1)
    t0f = t0.astype(F32)
    qp = []
    for h in range(A_HEADS):
        sl = _alibi_slope(h)
        qext = jnp.where(lane_e == 0, CHUNK * sl, jnp.where(lane_e == 1, sl, jnp.where(lane_e == 2, -sl * t0f, 0.0)))
        qp.append(jnp.concatenate([qlat[:, h * KV_RANK:(h + 1) * KV_RANK], qext], 1))
    qp = jnp.concatenate(qp, 0).astype(BF16)

    m_ref[...] = jnp.full(m_ref.shape, NEG_INF, F32)
    acc_ref[...] = jnp.zeros(acc_ref.shape, F32)

    def attend(j, slot, last=False):
        kt = key_rows(kp_ref, j)
        logits = _dot_nt(qp, kt)
        mb = mask_bias(j)
        if last:
            row_t = t0 + lax.broadcasted_iota(I32, (Q_BLOCK, KEY_TILE), 0)
            key_s = n_full * KEY_TILE + lax.broadcasted_iota(I32, (Q_BLOCK, KEY_TILE), 1)
            mb = jnp.where((key_s // CHUNK) <= (row_t // CHUNK), mb, NEG_INF)
            ahead = jnp.maximum(key_s - row_t, 0).astype(F32)
        for h in range(A_HEADS):
            l = logits[h * Q_BLOCK:(h + 1) * Q_BLOCK] + mb
            if last:
                l = l - (2.0 * _alibi_slope(h)) * ahead
            m_old = m_ref[h]
            m_new = jnp.maximum(m_old, jnp.broadcast_to(jnp.max(l, axis=1, keepdims=True), (Q_BLOCK, LANES)))
            m_ref[h] = m_new
            alpha = jnp.exp(m_old - m_new)
            p = jnp.exp(l - jnp.concatenate([m_new] * (KEY_TILE // LANES), 1))
            p_ref[slot, h * Q_BLOCK:(h + 1) * Q_BLOCK, :] = p.astype(BF16)
            acc_ref[h] = acc_ref[h] * jnp.concatenate([alpha, alpha], 1)
        pv = _dot(p_ref[slot], kt)
        for h in range(A_HEADS):
            acc_ref[h] = acc_ref[h] + pv[h * Q_BLOCK:(h + 1) * Q_BLOCK]

    def attend_group(width, first):
        def body(jj, c):
            for k in range(width):
                attend(first + width * jj + k, k % 2)
            return c
        return body

    done = 0
    for width in ATTEND_UNROLLS:
        trips = (n_full - done) // width
        lax.fori_loop(0, trips, attend_group(width, done), 0)
        done = done + trips * width
    attend(n_full, 0, last=True)

    outs = []
    for h in range(A_HEADS):
        a = acc_ref[h]
        outs.append(a[:, :KV_RANK] / a[:, KV_RANK + 2:KV_RANK + 3])
    olat = jnp.concatenate(outs, 1).astype(BF16)
    o_ref[0] = _dot(olat, wuv_ref[...]).astype(BF16)


def _dsa_attention(qidx, widx, qa, krep, ckv, w_uk, w_uv, bsz, seq):
    nq = seq // Q_BLOCK
    n_key_tiles = seq // KEY_TILE
    pos = lax.broadcasted_iota(I32, (seq, LANES), 0)
    lane = lax.broadcasted_iota(I32, (seq, LANES), 1)
    ext = jnp.where(lane == 0, pos // CHUNK, jnp.where(lane == 1, pos % CHUNK, jnp.where(lane == 2, 1, 0)))
    ext = ext.astype(BF16)
    eye = jnp.eye(A_HEADS, dtype=F32)
    wuk_bd = jnp.einsum("hg,hrd->hdgr", eye, w_uk).reshape(A_HEADS * A_HEAD_DIM, A_HEADS * KV_RANK)
    wuv_bd = jnp.einsum("hg,hrd->hrgd", eye, w_uv).reshape(A_HEADS * KV_RANK, A_HEADS * A_HEAD_DIM)
    blk = lambda b, i: (b, i, 0)
    whole = lambda b, i: (b, 0, 0)
    fixed = lambda b, i: (0, 0)
    r3 = lambda a: a.reshape(bsz, seq, a.shape[-1])
    return pl.pallas_call(
        _attn_kernel,
        grid=(bsz, nq),
        in_specs=[pl.BlockSpec((1, Q_BLOCK, 256), blk), pl.BlockSpec((1, Q_BLOCK, LANES), blk),
                  pl.BlockSpec((1, Q_BLOCK, 512), blk),
                  pl.BlockSpec((1, seq, 256), whole), pl.BlockSpec((1, seq, KV_RANK), whole),
                  pl.BlockSpec((seq, LANES), fixed),
                  pl.BlockSpec((512, 1024), fixed), pl.BlockSpec((1024, 512), fixed)],
        out_specs=pl.BlockSpec((1, Q_BLOCK, 512), blk),
        out_shape=jax.ShapeDtypeStruct((bsz, seq, 512), BF16),
        scratch_shapes=[pltpu.VMEM((seq, 2 * KV_RANK), BF16),
                        pltpu.VMEM((n_key_tiles + 1, KEY_TILE, Q_BLOCK), F32),
                        pltpu.VMEM((2, A_HEADS * Q_BLOCK, KEY_TILE), BF16),
                        pltpu.VMEM((A_HEADS, Q_BLOCK, 2 * KV_RANK), F32),
                        pltpu.VMEM((A_HEADS, Q_BLOCK, LANES), F32)],
        compiler_params=_cparams(("arbitrary", "arbitrary")),
        name="attn",
    )(r3(qidx), r3(widx), r3(qa), r3(krep), r3(ckv), ext, wuk_bd.astype(BF16), wuv_bd.astype(BF16))


CONV_HALO = 32
CONV_PHASES = 8


def _conv_kernel(cur_ref, prev_ref, w_ref, b_ref, g_ref, beta_ref, o_ref, win_ref, shift_ref):
    j = pl.program_id(1)
    tq = cur_ref.shape[1]
    halo = prev_ref[0, tq - CONV_HALO:, :]
    win_ref[:CONV_HALO] = jnp.where(j > 0, halo, jnp.zeros_like(halo))
    win_ref[CONV_HALO:] = cur_ref[0]
    acc = jnp.zeros((tq, cur_ref.shape[2]), F32)
    first = CONV_HALO - (CONV_WIDTH - 1)
    for r in range(CONV_PHASES):
        taps = range(r, CONV_WIDTH, CONV_PHASES)
        rows = taps[-1] - r + tq
        shift_ref[r, :rows] = win_ref[first + r:first + r + rows, :]
        for k in taps:
            acc = acc + shift_ref[r, k - r:k - r + tq] * w_ref[k:k + 1, :]
    y = _layer_norm(acc + b_ref[...], g_ref[...], beta_ref[...])
    o_ref[0] = (y * jax.nn.sigmoid(y)).astype(BF16)


def _conv(u, w_dw, b_dw, ln_g, ln_b, bsz, seq, tq=512):
    c = u.shape[-1]
    u3 = u.reshape(bsz, seq, c)
    fixed = lambda b, j: (0, 0)
    return pl.pallas_call(
        _conv_kernel,
        grid=(bsz, seq // tq),
        in_specs=[pl.BlockSpec((1, tq, c), lambda b, j: (b, j, 0)),
                  pl.BlockSpec((1, tq, c), lambda b, j: (b, jnp.maximum(j - 1, 0), 0)),
                  pl.BlockSpec((CONV_WIDTH, c), fixed), pl.BlockSpec((1, c), fixed),
                  pl.BlockSpec((1, c), fixed), pl.BlockSpec((1, c), fixed)],
        out_specs=pl.BlockSpec((1, tq, c), lambda b, j: (b, j, 0)),
        out_shape=jax.ShapeDtypeStruct((bsz, seq, c), BF16),
        scratch_shapes=[pltpu.VMEM((tq + CONV_HALO, c), F32),
                        pltpu.VMEM((CONV_PHASES, tq + CONV_HALO, c), F32)],
        compiler_params=_cparams(("arbitrary", "arbitrary")),
        name="conv",
    )(u3, u3, w_dw, b_dw.reshape(1, c), ln_g.reshape(1, c), ln_b.reshape(1, c))


def _memattn_kernel(q_ref, mem_ref, wk_ref, wv_ref, o_ref, k_ref, v_ref):
    @pl.when(pl.program_id(1) == 0)
    def _():
        mb = mem_ref[0].astype(BF16)
        k_ref[...] = _dot(mb, wk_ref[...]).astype(BF16)
        v_ref[...] = _dot(mb, wv_ref[...]).astype(BF16)

    q = q_ref[0]
    outs = []
    for h in range(C_HEADS):
        sl = slice(h * C_HEAD_DIM, (h + 1) * C_HEAD_DIM)
        s = _dot_nt(q[:, sl], k_ref[:, sl]) * C_SCALE
        e = jnp.exp(s - jnp.max(s, axis=1, keepdims=True))
        p = e / jnp.sum(e, axis=1, keepdims=True)
        outs.append(_dot(p.astype(BF16), v_ref[:, sl]))
    o_ref[0] = jnp.concatenate(outs, 1).astype(BF16)


def _mem_attention(qc, mem, w_mem_k, w_mem_v, bsz, seq, tq=512):
    n_mem, dm = mem.shape[1:]
    cw = C_HEADS * C_HEAD_DIM
    fixed = lambda b, j: (0, 0)
    return pl.pallas_call(
        _memattn_kernel,
        grid=(bsz, seq // tq),
        in_specs=[pl.BlockSpec((1, tq, cw), lambda b, j: (b, j, 0)),
                  pl.BlockSpec((1, n_mem, dm), lambda b, j: (b, 0, 0)),
                  pl.BlockSpec((dm, cw), fixed), pl.BlockSpec((dm, cw), fixed)],
        out_specs=pl.BlockSpec((1, tq, cw), lambda b, j: (b, j, 0)),
        out_shape=jax.ShapeDtypeStruct((bsz, seq, cw), BF16),
        scratch_shapes=[pltpu.VMEM((n_mem, cw), BF16), pltpu.VMEM((n_mem, cw), BF16)],
        compiler_params=_cparams(("arbitrary", "arbitrary")),
        name="memattn",
    )(qc.reshape(bsz, seq, cw), mem, w_mem_k.astype(BF16), w_mem_v.astype(BF16))


def _mix_kernel(h0_ref, oa_ref, u2_ref, oc_ref, wg_ref, bg_ref, woa_ref, wpw_ref, bpw_ref, woc_ref,
                wout_ref, bout_ref, g_ref, b_ref, wr_ref, br_ref, h1_ref, rl_ref):
    h0 = h0_ref[...]
    hb = h0.astype(BF16)
    dm = h0.shape[1]
    y = (_dot(oa_ref[...], woa_ref[...]),
         _dot(u2_ref[...], wpw_ref[...]) + bpw_ref[...],
         _dot(oc_ref[...], woc_ref[...]))
    mixed = jnp.zeros_like(h0)
    for k in range(N_BRANCH):
        gate = jax.nn.sigmoid(_dot(hb, wg_ref[:, k * dm:(k + 1) * dm]) + bg_ref[:, k * dm:(k + 1) * dm])
        mixed = mixed + gate * y[k]
    out = _dot(mixed.astype(BF16), wout_ref[...]) + bout_ref[...]
    h1 = _layer_norm(DN_ALPHA * h0 + out, g_ref[...], b_ref[...])
    h1_ref[...] = h1
    rl_ref[...] = _dot(h1.astype(BF16), wr_ref[...]) + br_ref[...]


def _mix(h0, oa, u2, oc, w_gate, b_gate, w_o_a, w_pw2, b_pw2, w_o_c, w_out, b_out, ln_g, ln_b, w_router, b_router,
         tm=256):
    n, dm = h0.shape
    row = lambda i: (i, 0)
    fixed = lambda i: (0, 0)
    wr = jnp.zeros((dm, LANES), F32).at[:, :N_EXPERTS].set(w_router).astype(BF16)
    br = jnp.zeros((1, LANES), F32).at[0, :N_EXPERTS].set(b_router)
    full = lambda a: pl.BlockSpec(a.shape, fixed)
    ws = [w_gate.astype(BF16), b_gate.reshape(1, -1), w_o_a.astype(BF16), w_pw2.astype(BF16), b_pw2.reshape(1, -1),
          w_o_c.astype(BF16), w_out.astype(BF16), b_out.reshape(1, -1), ln_g.reshape(1, -1), ln_b.reshape(1, -1),
          wr, br]
    return pl.pallas_call(
        _mix_kernel,
        grid=(n // tm,),
        in_specs=[pl.BlockSpec((tm, dm), row), pl.BlockSpec((tm, 512), row), pl.BlockSpec((tm, 512), row),
                  pl.BlockSpec((tm, 512), row)] + [full(a) for a in ws],
        out_specs=[pl.BlockSpec((tm, dm), row), pl.BlockSpec((tm, LANES), row)],
        out_shape=[jax.ShapeDtypeStruct((n, dm), F32), jax.ShapeDtypeStruct((n, LANES), F32)],
        compiler_params=_cparams(("arbitrary",)),
        name="mix",
    )(h0, oa, u2, oc, *ws)


def _route_kernel(rl_ref, eid_ref, gate_ref, rank_ref, cnt_ref, carry_ref):
    @pl.when(pl.program_id(0) == 0)
    def _():
        carry_ref[...] = jnp.zeros_like(carry_ref)

    tm = rl_ref.shape[0]
    lane = lax.broadcasted_iota(I32, (tm, LANES), 1)
    vals = jnp.where(lane < N_EXPERTS, rl_ref[...], -jnp.inf)
    top_v, top_i, sels = [], [], []
    for _ in range(TOP_K):
        m = jnp.max(vals, axis=1, keepdims=True)
        idx = jnp.min(jnp.where(vals == m, lane, LANES), axis=1, keepdims=True)
        sel = lane == idx
        top_v.append(m)
        top_i.append(idx)
        sels.append(sel)
        vals = jnp.where(sel, -jnp.inf, vals)
    es = [jnp.exp(v - top_v[0]) for v in top_v]
    denom = es[0] + es[1] + es[2] + es[3]
    onehot = jnp.zeros((tm, LANES), F32)
    for sel in sels:
        onehot = onehot + jnp.where(sel, 1.0, 0.0)
    r = lax.broadcasted_iota(I32, (tm, tm), 0)
    c = lax.broadcasted_iota(I32, (tm, tm), 1)
    tri = jnp.where(c < r, 1.0, 0.0).astype(BF16)
    before = _dot(tri, onehot.astype(BF16)) + carry_ref[...]
    carry = carry_ref[...] + jnp.sum(onehot, axis=0, keepdims=True)
    carry_ref[...] = carry
    cnt_ref[...] = carry
    eid = jnp.zeros((tm, LANES), I32)
    gate = jnp.zeros((tm, LANES), F32)
    rank = jnp.zeros((tm, LANES), F32)
    for k in range(TOP_K):
        eid = jnp.where(lane == k, top_i[k], eid)
        gate = jnp.where(lane == k, es[k] / denom, gate)
        rank = jnp.where(lane == k, jnp.sum(jnp.where(sels[k], before, 0.0), axis=1, keepdims=True), rank)
    eid_ref[...] = jnp.transpose(eid)[:ROUTE_ROWS]
    gate_ref[...] = gate
    rank_ref[...] = jnp.transpose(rank.astype(I32))[:ROUTE_ROWS]


ROUTE_ROWS = 8


def _route(rl, tm=512):
    n = rl.shape[0]
    row = lambda i: (i, 0)
    col = lambda i: (0, i)
    return pl.pallas_call(
        _route_kernel,
        grid=(n // tm,),
        in_specs=[pl.BlockSpec((tm, LANES), row)],
        out_specs=[pl.BlockSpec((ROUTE_ROWS, tm), col), pl.BlockSpec((tm, LANES), row),
                   pl.BlockSpec((ROUTE_ROWS, tm), col), pl.BlockSpec((1, LANES), lambda i: (0, 0))],
        out_shape=[jax.ShapeDtypeStruct((ROUTE_ROWS, n), I32), jax.ShapeDtypeStruct((n, LANES), F32),
                   jax.ShapeDtypeStruct((ROUTE_ROWS, n), I32), jax.ShapeDtypeStruct((1, LANES), F32)],
        scratch_shapes=[pltpu.VMEM((1, LANES), F32)],
        compiler_params=_cparams(("arbitrary",)),
        name="route",
    )(rl)


SUBLANES = 8
DMA_UNROLL = 2


def _to_tiles(x, tile_ref):
    for s in range(SUBLANES):
        tile_ref[pl.ds(s, x.shape[0], stride=SUBLANES), :] = x[:, s * LANES:(s + 1) * LANES]


def _from_tiles(tile_ref):
    rows = tile_ref.shape[0] // SUBLANES
    return jnp.concatenate([tile_ref[pl.ds(s, rows, stride=SUBLANES), :] for s in range(SUBLANES)], axis=1)


def _tile_rows(ref, row):
    return ref.at[pl.ds(pl.multiple_of(row * SUBLANES, SUBLANES), SUBLANES)]


def _dispatch_kernel(dest_ref, h1_ref, xs_in_hbm, xs_hbm, stage_ref, sem):
    del xs_in_hbm
    tm = h1_ref.shape[0]
    _to_tiles(h1_ref[...], stage_ref)

    def start(t, c):
        for k in range(TOP_K):
            pltpu.make_async_copy(_tile_rows(stage_ref, t), _tile_rows(xs_hbm, dest_ref[k, t]),
                                  sem).start(priority=k % 2)
        return c

    def wait(t, c):
        for k in range(TOP_K):
            pltpu.make_async_copy(_tile_rows(stage_ref, 0), _tile_rows(xs_hbm, 0), sem).wait()
        return c

    lax.fori_loop(0, tm, start, 0, unroll=DMA_UNROLL)
    lax.fori_loop(0, tm, wait, 0, unroll=DMA_UNROLL)


def _dispatch(h1, dest_t, n_rows, tm=256):
    n, dm = h1.shape
    xs0 = jnp.zeros((n_rows * SUBLANES, LANES), F32)
    return pl.pallas_call(
        _dispatch_kernel,
        grid=(n // tm,),
        in_specs=[pl.BlockSpec((ROUTE_ROWS, tm), lambda i: (0, i), memory_space=pltpu.SMEM),
                  pl.BlockSpec((tm, dm), lambda i: (i, 0)), pl.BlockSpec(memory_space=pl.ANY)],
        out_specs=pl.BlockSpec(memory_space=pl.ANY),
        out_shape=jax.ShapeDtypeStruct((n_rows * SUBLANES, LANES), F32),
        scratch_shapes=[pltpu.VMEM((tm * SUBLANES, LANES), F32), pltpu.SemaphoreType.DMA(())],
        input_output_aliases={2: 0},
        compiler_params=_cparams(("arbitrary",)),
        name="dispatch",
    )(dest_t, h1, xs0)


def _experts_kernel(blk_exp_ref, n_used_ref, xs_ref, w1_ref, b1_ref, w2_ref, b2_ref, ys_ref, w1b_ref, w2b_ref):
    b = pl.program_id(0)
    prev = blk_exp_ref[jnp.maximum(b - 1, 0)]
    used = b < n_used_ref[0]

    @pl.when(used & ((b == 0) | (blk_exp_ref[b] != prev)))
    def _():
        w1b_ref[...] = w1_ref[0].astype(BF16)
        w2b_ref[...] = w2_ref[0].astype(BF16)

    @pl.when(used)
    def _():
        d_exp = w2b_ref.shape[0]
        hdn = _dot(_from_tiles(xs_ref).astype(BF16), w1b_ref[...]) + b1_ref[0]
        g = jnp.minimum(hdn[:, :d_exp], SWIGLU_LIMIT)
        u = jnp.clip(hdn[:, d_exp:], -SWIGLU_LIMIT, SWIGLU_LIMIT)
        act = (u + 1.0) * (g * jax.nn.sigmoid(SWIGLU_ALPHA * g))
        _to_tiles(_dot(act.astype(BF16), w2b_ref[...]) + b2_ref[0], ys_ref)

    @pl.when(jnp.logical_not(used))
    def _():
        ys_ref[...] = jnp.zeros_like(ys_ref)


def _experts(xs, blk_exp, n_used, w1, b1, w2, b2):
    n_rows = xs.shape[0] // SUBLANES
    n_exp, dm, two_f = w1.shape
    d_exp = two_f // 2
    grid_spec = pltpu.PrefetchScalarGridSpec(
        num_scalar_prefetch=2,
        grid=(n_rows // ROW_BLOCK,),
        in_specs=[pl.BlockSpec((ROW_BLOCK * SUBLANES, LANES), lambda b, be, nu: (b, 0)),
                  pl.BlockSpec((1, dm, two_f), lambda b, be, nu: (be[b], 0, 0)),
                  pl.BlockSpec((1, 1, two_f), lambda b, be, nu: (be[b], 0, 0)),
                  pl.BlockSpec((1, d_exp, dm), lambda b, be, nu: (be[b], 0, 0)),
                  pl.BlockSpec((1, 1, dm), lambda b, be, nu: (be[b], 0, 0))],
        out_specs=pl.BlockSpec((ROW_BLOCK * SUBLANES, LANES), lambda b, be, nu: (b, 0)),
        scratch_shapes=[pltpu.VMEM((dm, two_f), BF16), pltpu.VMEM((d_exp, dm), BF16)],
    )
    return pl.pallas_call(
        _experts_kernel,
        grid_spec=grid_spec,
        out_shape=jax.ShapeDtypeStruct((n_rows * SUBLANES, LANES), F32),
        compiler_params=_cparams(("arbitrary",)),
        name="experts",
    )(blk_exp, n_used, xs, w1, b1.reshape(n_exp, 1, two_f), w2, b2.reshape(n_exp, 1, dm))


def _combine_kernel(dest_ref, ys_hbm, h1_ref, gate_ref, g_ref, b_ref, o_ref, buf_ref, sem):
    tm = h1_ref.shape[0]

    def start(t, c):
        for k in range(TOP_K):
            pltpu.make_async_copy(_tile_rows(ys_hbm, dest_ref[k, t]), _tile_rows(buf_ref.at[k], t),
                                  sem).start(priority=k % 2)
        return c

    def wait(t, c):
        for k in range(TOP_K):
            pltpu.make_async_copy(_tile_rows(ys_hbm, 0), _tile_rows(buf_ref.at[0], 0), sem).wait()
        return c

    lax.fori_loop(0, tm, start, 0, unroll=DMA_UNROLL)
    lax.fori_loop(0, tm, wait, 0, unroll=DMA_UNROLL)
    gate = gate_ref[...]
    moe = jnp.zeros(h1_ref.shape, F32)
    for k in range(TOP_K):
        moe = moe + _from_tiles(buf_ref.at[k]) * gate[:, k:k + 1]
    o_ref[...] = _layer_norm(DN_ALPHA * h1_ref[...] + moe, g_ref[...], b_ref[...])


def _combine(ys, dest_t, h1, gate, ln_g, ln_b, tm=256):
    n, dm = h1.shape
    row = lambda i: (i, 0)
    fixed = lambda i: (0, 0)
    return pl.pallas_call(
        _combine_kernel,
        grid=(n // tm,),
        in_specs=[pl.BlockSpec((ROUTE_ROWS, tm), lambda i: (0, i), memory_space=pltpu.SMEM),
                  pl.BlockSpec(memory_space=pl.ANY),
                  pl.BlockSpec((tm, dm), row), pl.BlockSpec((tm, LANES), row),
                  pl.BlockSpec((1, dm), fixed), pl.BlockSpec((1, dm), fixed)],
        out_specs=pl.BlockSpec((tm, dm), row),
        out_shape=jax.ShapeDtypeStruct((n, dm), F32),
        scratch_shapes=[pltpu.VMEM((TOP_K, tm * SUBLANES, LANES), F32), pltpu.SemaphoreType.DMA(())],
        compiler_params=_cparams(("arbitrary",)),
        name="combine",
    )(dest_t, ys, h1, gate, ln_g.reshape(1, dm), ln_b.reshape(1, dm))


def _moe(h1, rl, w1, b1, w2, b2, ln_g, ln_b):
    n, dm = h1.shape
    eid_t, gate, rank_t, counts = _route(rl)
    counts = counts[0, :N_EXPERTS].astype(I32)
    padded = (counts + ROW_BLOCK - 1) // ROW_BLOCK * ROW_BLOCK
    pad_end = jnp.cumsum(padded)
    group_start = pad_end - padded
    n_blocks = n * TOP_K // ROW_BLOCK + N_EXPERTS
    n_used = (pad_end[-1] // ROW_BLOCK).astype(I32)
    blk = jnp.minimum(jnp.arange(n_blocks, dtype=I32), n_used - 1)
    blk_exp = jnp.sum((pad_end[None, :] <= (blk * ROW_BLOCK)[:, None]).astype(I32), axis=1)
    expert = jnp.arange(N_EXPERTS, dtype=I32)[:, None, None]
    dest_t = jnp.sum(jnp.where(eid_t[None] == expert, group_start[:, None, None], 0), axis=0) + rank_t
    xs = _dispatch(h1, dest_t, n_blocks * ROW_BLOCK)
    ys = _experts(xs, blk_exp, n_used.reshape(1), w1, b1, w2, b2)
    return _combine(ys, dest_t, h1, gate, ln_g, ln_b)


def kernel(x, mem, ln0_g, ln0_b, w_in, b_in, kv_norm_g, w_uk, w_uv, w_o_a, w_dw, b_dw, conv_ln_g, conv_ln_b, w_pw2, b_pw2, w_mem_k, w_mem_v, w_o_c, w_out, b_out, ln1_g, ln1_b, w_router, b_router, w1, b1, w2, b2, ln2_g, ln2_b):
    bsz, seq, dm = x.shape
    n = bsz * seq
    assert w_in.shape[0] == DEPTH == 1, "the stem LayerNorm is fused into the single layer's projection"
    l = 0
    gate_lo = w_in.shape[-1] - N_BRANCH * dm
    h0, qa, ckv, qidx, krep, widx, u, qc = _proj(x.reshape(n, dm), ln0_g, ln0_b, w_in[l, :, :gate_lo],
                                                   b_in[l, :gate_lo], kv_norm_g[l])
    oa = _dsa_attention(qidx, widx, qa, krep, ckv, w_uk[l], w_uv[l], bsz, seq).reshape(n, -1)
    u2 = _conv(u, w_dw[l], b_dw[l], conv_ln_g[l], conv_ln_b[l], bsz, seq).reshape(n, -1)
    oc = _mem_attention(qc, mem, w_mem_k[l], w_mem_v[l], bsz, seq).reshape(n, -1)
    h1, rl = _mix(h0, oa, u2, oc, w_in[l, :, gate_lo:], b_in[l, gate_lo:], w_o_a[l], w_pw2[l], b_pw2[l],
                  w_o_c[l], w_out[l], b_out[l], ln1_g[l], ln1_b[l], w_router[l], b_router[l])
    h2 = _moe(h1, rl, w1[l], b1[l], w2[l], b2[l], ln2_g[l], ln2_b[l])
    return h2.reshape(bsz, seq, dm)
```

```python
import jax
import jax.numpy as jnp
from jax import lax
from jax.experimental import pallas as pl
from jax.experimental.pallas import tpu as pltpu

F32 = jnp.float32
BF16 = jnp.bfloat16
I32 = jnp.int32

CHUNK = 64
A_HEADS = 8
A_HEAD_DIM = 64
KV_RANK = 128
IDX_HEADS = 8
IDX_DIM = 32
TOPK_MAX = 256
CONV_WIDTH = 31
C_HEADS = 4
C_HEAD_DIM = 128
N_BRANCH = 3
N_EXPERTS = 32
TOP_K = 4
SWIGLU_LIMIT = 7.0
SWIGLU_ALPHA = 1.702
LN_EPS = 1e-5
NEG_INF = -1e30
DEPTH = 1
DN_ALPHA = (2 * DEPTH) ** 0.25
A_SCALE = A_HEAD_DIM ** -0.5
IDX_SCALE = (IDX_HEADS * IDX_DIM) ** -0.5
C_SCALE = C_HEAD_DIM ** -0.5

LANES = 128
SUBLANES = 8
Q_BLOCK = 128
KEY_TILE = 512
ATTEND_UNROLLS = (4, 2, 1)
COARSE_BITS = 20
FINE_KEEP = 4
FINE_CHAINS = 4
ROW_BLOCK = 512
VMEM_LIMIT = 56 * 1024 * 1024


def _cparams(sem, vmem=VMEM_LIMIT):
    return pltpu.CompilerParams(dimension_semantics=sem, vmem_limit_bytes=vmem)


def _layer_norm(x, g, b):
    xc = x - jnp.mean(x, -1, keepdims=True)
    var = jnp.mean(xc * xc, -1, keepdims=True)
    return xc * lax.rsqrt(var + LN_EPS) * g + b


def _dot(a, b):
    return jnp.dot(a, b, preferred_element_type=F32)


def _dot_nt(a, b):
    return lax.dot_general(a, b, (((1,), (1,)), ((), ())), preferred_element_type=F32)


_P_QA = (0, 512)
_P_CKV = (512, 640)
_P_QIDX = (640, 896)
_P_KREP = (896, 1152)
_P_WIDX = (1152, 1280)
_P_GLU_A = (1280, 1792)
_P_GLU_G = (1792, 2304)
_P_QC = (2304, 2816)
_P_COLS = 2816


def _proj_kernel(x_ref, g_ref, b_ref, w_ref, bias_ref, kvg_ref,
                 h0_ref, qa_ref, ckv_ref, qidx_ref, krep_ref, widx_ref, u_ref, qc_ref):
    h = _layer_norm(x_ref[...], g_ref[...], b_ref[...])
    h0_ref[...] = h
    hb = h.astype(BF16)

    def proj(span):
        lo, hi = span
        return _dot(hb, w_ref[:, lo:hi]) + bias_ref[:, lo:hi]

    qa_ref[...] = proj(_P_QA).astype(BF16)
    ckv = proj(_P_CKV)
    ckv = ckv * lax.rsqrt(jnp.mean(ckv * ckv, -1, keepdims=True) + LN_EPS) * kvg_ref[...]
    ckv_ref[...] = ckv.astype(BF16)
    qidx_ref[...] = proj(_P_QIDX).astype(BF16)
    krep_ref[...] = proj(_P_KREP).astype(BF16)
    widx_ref[...] = proj(_P_WIDX) * IDX_SCALE
    u_ref[...] = proj(_P_GLU_A) * jax.nn.sigmoid(proj(_P_GLU_G))
    qc_ref[...] = proj(_P_QC).astype(BF16)


def _proj(xt, ln_g, ln_b, w_in, b_in, kv_g, tm=512):
    n, dm = xt.shape
    o = 0
    parts = {}
    for name, width in (("qa", 512), ("ckv", 128), ("qidx", 256), ("kidx", 32), ("widx", 8),
                        ("glu", 1024), ("qc", 512)):
        parts[name] = (o, o + width)
        o += width

    def cols(a, name):
        lo, hi = parts[name]
        return a[..., lo:hi]

    def pack(a):
        pad = jnp.zeros(a.shape[:-1] + (LANES - 8,), a.dtype)
        glu = cols(a, "glu")
        return jnp.concatenate(
            [cols(a, "qa"), cols(a, "ckv"), cols(a, "qidx"),
             jnp.concatenate([cols(a, "kidx")] * IDX_HEADS, -1),
             cols(a, "widx"), pad, glu[..., :512], glu[..., 512:], cols(a, "qc")], -1)

    w = pack(w_in).astype(BF16)
    bias = pack(b_in).reshape(1, _P_COLS)
    row = lambda i: (i, 0)
    fixed = lambda i: (0, 0)
    outs = [(dm, F32), (512, BF16), (128, BF16), (256, BF16), (256, BF16), (128, F32), (512, F32), (512, BF16)]
    return pl.pallas_call(
        _proj_kernel,
        grid=(n // tm,),
        in_specs=[pl.BlockSpec((tm, dm), row), pl.BlockSpec((1, dm), fixed), pl.BlockSpec((1, dm), fixed),
                  pl.BlockSpec((dm, _P_COLS), fixed), pl.BlockSpec((1, _P_COLS), fixed),
                  pl.BlockSpec((1, KV_RANK), fixed)],
        out_specs=[pl.BlockSpec((tm, c), row) for c, _ in outs],
        out_shape=[jax.ShapeDtypeStruct((n, c), d) for c, d in outs],
        compiler_params=_cparams(("arbitrary",)),
        name="proj",
    )(xt, ln_g.reshape(1, dm), ln_b.reshape(1, dm), w, bias, kv_g.reshape(1, KV_RANK))


def _alibi_slope(h):
    return 2.0 ** (-(8.0 / A_HEADS) * (h + 1))


def _attn_kernel(qidx_ref, widx_ref, qa_ref, krep_ref, ckv_ref, ext_ref, wuk_ref, wuv_ref, o_ref,
                 kp_ref, sc_ref, p_ref, acc_ref, m_ref):
    i = pl.program_id(1)
    t0 = i * Q_BLOCK

    @pl.when(i == 0)
    def _():
        kp_ref[:, :KV_RANK] = ckv_ref[0]
        kp_ref[:, KV_RANK:] = ext_ref[...]

    n_full = i // (KEY_TILE // Q_BLOCK)
    n_tiles = n_full + 1

    def key_rows(ref, j):
        return ref[pl.ds(pl.multiple_of(j * KEY_TILE, KEY_TILE), KEY_TILE), :]

    qi = qidx_ref[0]
    head_of_lane = lax.broadcasted_iota(I32, (1, IDX_HEADS * IDX_DIM), 1) // IDX_DIM
    qbd = jnp.concatenate([jnp.where(head_of_lane == h, qi, jnp.zeros_like(qi)) for h in range(IDX_HEADS)], 0)
    w_t = jnp.transpose(widx_ref[0])

    def score_tile(j):
        s = _dot_nt(key_rows(krep_ref.at[0], j), qbd)
        acc = jnp.zeros((KEY_TILE, Q_BLOCK), F32)
        for h in range(IDX_HEADS):
            acc = acc + jnp.maximum(s[:, h * Q_BLOCK:(h + 1) * Q_BLOCK], 0.0) * w_t[h:h + 1, :]
        return acc + 0.0

    def score_pair(jj, c):
        sc_ref[2 * jj] = score_tile(2 * jj)
        sc_ref[2 * jj + 1] = score_tile(2 * jj + 1)
        return c

    def score_single(j, c):
        sc_ref[j] = score_tile(j)
        return c

    lax.fori_loop(0, n_full // 2, score_pair, 0)
    lax.fori_loop(n_full // 2 * 2, n_full, score_single, 0)
    key_row = lax.broadcasted_iota(I32, (KEY_TILE, Q_BLOCK), 0)
    q_pos = t0 + lax.broadcasted_iota(I32, (KEY_TILE, Q_BLOCK), 1)
    key_last = n_full * KEY_TILE + key_row
    adm_last = (key_last // CHUNK) <= (q_pos // CHUNK)
    sc_ref[n_full] = jnp.where(adm_last, score_tile(n_full), -jnp.inf)

    def by_sublane(a):
        return a.reshape(KEY_TILE // SUBLANES, SUBLANES, Q_BLOCK)

    def scan_tiles(step, carry):
        carry = lax.fori_loop(0, n_tiles // 2, lambda jj, c: step(2 * jj + 1, step(2 * jj, c)), carry)
        return lax.fori_loop(n_tiles // 2 * 2, n_tiles, step, carry)

    def count_where(*preds):
        def step(j, cnts):
            s = sc_ref[j]
            return tuple(cnt + jnp.sum(by_sublane(jnp.where(pred(s, j * KEY_TILE), 1, 0)), axis=0)
                         for cnt, pred in zip(cnts, preds))
        cnts = scan_tiles(step, (jnp.zeros((SUBLANES, Q_BLOCK), I32),) * len(preds))
        cnts = [jnp.sum(c, axis=0, keepdims=True) for c in cnts]
        return cnts[0] if len(preds) == 1 else cnts

    def key_to_float(key):
        bits = key ^ ((key >> 31) & jnp.int32(0x7FFFFFFF))
        return lax.bitcast_convert_type(bits, F32)

    int_min = jnp.int32(-2 ** 31)

    def bit_step(b, prefix):
        cand = prefix + lax.shift_left(jnp.int32(1), jnp.int32(31) - b)
        cand_f = key_to_float(cand)
        return jnp.where(count_where(lambda s, base: s >= cand_f) >= TOPK_MAX, cand, prefix)

    coarse = lax.fori_loop(0, COARSE_BITS, bit_step, jnp.full((1, Q_BLOCK), int_min, I32))
    enough = coarse != int_min

    top_f = key_to_float(coarse + jnp.int32(1 << (32 - COARSE_BITS)))

    def insert(best, v):
        out = []
        for b in best:
            out.append(jnp.maximum(b, v))
            v = jnp.minimum(b, v)
        return out

    def merge(best, other):
        for v in other:
            best = insert(best, v)
        return best

    def fine_step(j, carry):
        cnt, lists = carry[0], [list(carry[1 + FINE_KEEP * c:1 + FINE_KEEP * (c + 1)]) for c in range(FINE_CHAINS)]
        s = sc_ref[j]
        above = s >= top_f
        cnt = cnt + jnp.sum(by_sublane(jnp.where(above, 1, 0)), axis=0)
        below = by_sublane(jnp.where(above, -jnp.inf, s))
        for g in range(KEY_TILE // SUBLANES):
            lists[g % FINE_CHAINS] = insert(lists[g % FINE_CHAINS], below[g])
        return (cnt,) + tuple(v for lst in lists for v in lst)

    lowest = jnp.full((SUBLANES, Q_BLOCK), -jnp.inf, F32)
    carry = scan_tiles(fine_step, (jnp.zeros((SUBLANES, Q_BLOCK), I32),) + (lowest,) * (FINE_KEEP * FINE_CHAINS))
    n_above = jnp.sum(carry[0], axis=0, keepdims=True)
    best = list(carry[1:1 + FINE_KEEP])
    for c in range(1, FINE_CHAINS):
        best = merge(best, carry[1 + FINE_KEEP * c:1 + FINE_KEEP * (c + 1)])
    top = [b[0:1] for b in best]
    for sub in range(1, SUBLANES):
        top = merge(top, [b[sub:sub + 1] for b in best])
    r = TOPK_MAX - n_above
    thr_fast = top[FINE_KEEP - 1]
    for k in range(FINE_KEEP - 1, 0, -1):
        thr_fast = jnp.where(r == k, top[k - 1], thr_fast)
    resolved = jnp.min(jnp.where((r <= FINE_KEEP) | jnp.logical_not(enough), 1, 0)) > 0

    def remaining_bits(_):
        return key_to_float(lax.fori_loop(COARSE_BITS, 32, bit_step, coarse))

    thr = lax.cond(resolved, lambda _: thr_fast, remaining_bits, 0)
    thr = jnp.where(enough, thr, -jnp.inf)

    n_gt, n_eq = count_where(lambda s, base: s > thr, lambda s, base: s == thr)
    need = TOPK_MAX - n_gt
    seq_bits = 13

    def tie_search(_):
        def step(b, lo):
            cand = lo + lax.shift_left(jnp.int32(1), jnp.int32(seq_bits) - b)
            below = count_where(lambda s, base: (s == thr) & ((base + key_row) < cand))
            return jnp.where(below < need, cand, lo)
        return lax.fori_loop(0, seq_bits + 1, step, jnp.zeros((1, Q_BLOCK), I32))

    overflow = jnp.max(jnp.where(n_gt + n_eq > TOPK_MAX, 1, 0)) > 0
    last_tie = lax.cond(overflow, tie_search, lambda _: jnp.full((1, Q_BLOCK), 2 ** 30, I32), 0)

    def mask_bias(j):
        s = sc_ref[j]
        tie_ok = jnp.where((j * KEY_TILE + key_row) <= last_tie, 0.0, NEG_INF)
        mb = jnp.where(s > thr, 0.0, jnp.where(s == thr, tie_ok, NEG_INF))
        return jnp.transpose(mb)

    qlat = _dot(qa_ref[0], wuk_ref[...]) * A_SCALE
    lane_e = lax.broadcasted_iota(I32, (Q_BLOCK, LANES), 1)
    t0f = t0.astype(F32)
    qp = []
    for h in range(A_HEADS):
        sl = _alibi_slope(h)
        qext = jnp.where(lane_e == 0, CHUNK * sl, jnp.where(lane_e == 1, sl, jnp.where(lane_e == 2, -sl * t0f, 0.0)))
        qp.append(jnp.concatenate([qlat[:, h * KV_RANK:(h + 1) * KV_RANK], qext], 1))
    qp = jnp.concatenate(qp, 0).astype(BF16)

    m_ref[...] = jnp.full(m_ref.shape, NEG_INF, F32)
    acc_ref[...] = jnp.zeros(acc_ref.shape, F32)

    def attend(j, slot, last=False):
        kt = key_rows(kp_ref, j)
        logits = _dot_nt(qp, kt)
        mb = mask_bias(j)
        if last:
            row_t = t0 + lax.broadcasted_iota(I32, (Q_BLOCK, KEY_TILE), 0)
            key_s = n_full * KEY_TILE + lax.broadcasted_iota(I32, (Q_BLOCK, KEY_TILE), 1)
            mb = jnp.where((key_s // CHUNK) <= (row_t // CHUNK), mb, NEG_INF)
            ahead = jnp.maximum(key_s - row_t, 0).astype(F32)
        alphas = []
        for h in range(A_HEADS):
            l = logits[h * Q_BLOCK:(h + 1) * Q_BLOCK] + mb
            if last:
                l = l - (2.0 * _alibi_slope(h)) * ahead
            m_old = m_ref[h]
            m_new = jnp.maximum(m_old, jnp.broadcast_to(jnp.max(l, axis=1, keepdims=True), (Q_BLOCK, LANES)))
            m_ref[h] = m_new
            alphas.append(jnp.exp(m_old - m_new))
            p = jnp.exp(l - jnp.concatenate([m_new] * (KEY_TILE // LANES), 1))
            p_ref[slot, h * Q_BLOCK:(h + 1) * Q_BLOCK, :] = p.astype(BF16)
        pv = _dot(p_ref[slot], kt)
        for h in range(A_HEADS):
            acc_ref[h] = acc_ref[h] * jnp.concatenate([alphas[h]] * 2, 1) + pv[h * Q_BLOCK:(h + 1) * Q_BLOCK]

    def attend_group(width, first):
        def body(jj, c):
            for k in range(width):
                attend(first + width * jj + k, k % 2)
            return c
        return body

    done = 0
    for width in ATTEND_UNROLLS:
        trips = (n_full - done) // width
        lax.fori_loop(0, trips, attend_group(width, done), 0)
        done = done + trips * width
    attend(n_full, 0, last=True)

    outs = []
    for h in range(A_HEADS):
        a = acc_ref[h]
        outs.append(a[:, :KV_RANK] / a[:, KV_RANK + 2:KV_RANK + 3])
    olat = jnp.concatenate(outs, 1).astype(BF16)
    o_ref[0] = _dot(olat, wuv_ref[...]).astype(BF16)


def _dsa_attention(qidx, widx, qa, krep, ckv, w_uk, w_uv, bsz, seq):
    nq = seq // Q_BLOCK
    n_key_tiles = seq // KEY_TILE
    pos = lax.broadcasted_iota(I32, (seq, LANES), 0)
    lane = lax.broadcasted_iota(I32, (seq, LANES), 1)
    ext = jnp.where(lane == 0, pos // CHUNK, jnp.where(lane == 1, pos % CHUNK, jnp.where(lane == 2, 1, 0)))
    ext = ext.astype(BF16)
    eye = jnp.eye(A_HEADS, dtype=F32)
    wuk_bd = jnp.einsum("hg,hrd->hdgr", eye, w_uk).reshape(A_HEADS * A_HEAD_DIM, A_HEADS * KV_RANK)
    wuv_bd = jnp.einsum("hg,hrd->hrgd", eye, w_uv).reshape(A_HEADS * KV_RANK, A_HEADS * A_HEAD_DIM)
    blk = lambda b, i: (b, i, 0)
    whole = lambda b, i: (b, 0, 0)
    fixed = lambda b, i: (0, 0)
    r3 = lambda a: a.reshape(bsz, seq, a.shape[-1])
    return pl.pallas_call(
        _attn_kernel,
        grid=(bsz, nq),
        in_specs=[pl.BlockSpec((1, Q_BLOCK, 256), blk), pl.BlockSpec((1, Q_BLOCK, LANES), blk),
                  pl.BlockSpec((1, Q_BLOCK, 512), blk),
                  pl.BlockSpec((1, seq, 256), whole), pl.BlockSpec((1, seq, KV_RANK), whole),
                  pl.BlockSpec((seq, LANES), fixed),
                  pl.BlockSpec((512, 1024), fixed), pl.BlockSpec((1024, 512), fixed)],
        out_specs=pl.BlockSpec((1, Q_BLOCK, 512), blk),
        out_shape=jax.ShapeDtypeStruct((bsz, seq, 512), BF16),
        scratch_shapes=[pltpu.VMEM((seq, 2 * KV_RANK), BF16),
                        pltpu.VMEM((n_key_tiles, KEY_TILE, Q_BLOCK), F32),
                        pltpu.VMEM((2, A_HEADS * Q_BLOCK, KEY_TILE), BF16),
                        pltpu.VMEM((A_HEADS, Q_BLOCK, 2 * KV_RANK), F32),
                        pltpu.VMEM((A_HEADS, Q_BLOCK, LANES), F32)],
        compiler_params=_cparams(("arbitrary", "arbitrary")),
        name="attn",
    )(r3(qidx), r3(widx), r3(qa), r3(krep), r3(ckv), ext, wuk_bd.astype(BF16), wuv_bd.astype(BF16))


CONV_HALO = 32
CONV_PHASES = SUBLANES


def _conv_kernel(cur_ref, prev_ref, w_ref, b_ref, g_ref, beta_ref, o_ref, win_ref, shift_ref):
    j = pl.program_id(1)
    tq = cur_ref.shape[1]
    halo = prev_ref[0, tq - CONV_HALO:, :]
    win_ref[:CONV_HALO] = jnp.where(j > 0, halo, jnp.zeros_like(halo))
    win_ref[CONV_HALO:] = cur_ref[0]
    acc = jnp.zeros((tq, cur_ref.shape[2]), F32)
    first = CONV_HALO - (CONV_WIDTH - 1)
    for r in range(CONV_PHASES):
        taps = range(r, CONV_WIDTH, CONV_PHASES)
        rows = taps[-1] - r + tq
        shift_ref[r, :rows] = win_ref[first + r:first + r + rows, :]
        for k in taps:
            acc = acc + shift_ref[r, k - r:k - r + tq] * w_ref[k:k + 1, :]
    y = _layer_norm(acc + b_ref[...], g_ref[...], beta_ref[...])
    o_ref[0] = (y * jax.nn.sigmoid(y)).astype(BF16)


def _conv(u, w_dw, b_dw, ln_g, ln_b, bsz, seq, tq=512):
    c = u.shape[-1]
    u3 = u.reshape(bsz, seq, c)
    fixed = lambda b, j: (0, 0)
    return pl.pallas_call(
        _conv_kernel,
        grid=(bsz, seq // tq),
        in_specs=[pl.BlockSpec((1, tq, c), lambda b, j: (b, j, 0)),
                  pl.BlockSpec((1, tq, c), lambda b, j: (b, jnp.maximum(j - 1, 0), 0)),
                  pl.BlockSpec((CONV_WIDTH, c), fixed), pl.BlockSpec((1, c), fixed),
                  pl.BlockSpec((1, c), fixed), pl.BlockSpec((1, c), fixed)],
        out_specs=pl.BlockSpec((1, tq, c), lambda b, j: (b, j, 0)),
        out_shape=jax.ShapeDtypeStruct((bsz, seq, c), BF16),
        scratch_shapes=[pltpu.VMEM((tq + CONV_HALO, c), F32),
                        pltpu.VMEM((CONV_PHASES, tq + CONV_HALO, c), F32)],
        compiler_params=_cparams(("arbitrary", "arbitrary")),
        name="conv",
    )(u3, u3, w_dw, b_dw.reshape(1, c), ln_g.reshape(1, c), ln_b.reshape(1, c))


def _memattn_kernel(q_ref, mem_ref, wk_ref, wv_ref, o_ref, k_ref, v_ref):
    @pl.when(pl.program_id(1) == 0)
    def _():
        mb = mem_ref[0].astype(BF16)
        k_ref[...] = _dot(mb, wk_ref[...]).astype(BF16)
        v_ref[...] = _dot(mb, wv_ref[...]).astype(BF16)

    q = q_ref[0]
    outs = []
    for h in range(C_HEADS):
        sl = slice(h * C_HEAD_DIM, (h + 1) * C_HEAD_DIM)
        s = _dot_nt(q[:, sl], k_ref[:, sl]) * C_SCALE
        e = jnp.exp(s - jnp.max(s, axis=1, keepdims=True))
        p = e / jnp.sum(e, axis=1, keepdims=True)
        outs.append(_dot(p.astype(BF16), v_ref[:, sl]))
    o_ref[0] = jnp.concatenate(outs, 1).astype(BF16)


def _mem_attention(qc, mem, w_mem_k, w_mem_v, bsz, seq, tq=512):
    n_mem, dm = mem.shape[1:]
    cw = C_HEADS * C_HEAD_DIM
    fixed = lambda b, j: (0, 0)
    return pl.pallas_call(
        _memattn_kernel,
        grid=(bsz, seq // tq),
        in_specs=[pl.BlockSpec((1, tq, cw), lambda b, j: (b, j, 0)),
                  pl.BlockSpec((1, n_mem, dm), lambda b, j: (b, 0, 0)),
                  pl.BlockSpec((dm, cw), fixed), pl.BlockSpec((dm, cw), fixed)],
        out_specs=pl.BlockSpec((1, tq, cw), lambda b, j: (b, j, 0)),
        out_shape=jax.ShapeDtypeStruct((bsz, seq, cw), BF16),
        scratch_shapes=[pltpu.VMEM((n_mem, cw), BF16), pltpu.VMEM((n_mem, cw), BF16)],
        compiler_params=_cparams(("arbitrary", "arbitrary")),
        name="memattn",
    )(qc.reshape(bsz, seq, cw), mem, w_mem_k.astype(BF16), w_mem_v.astype(BF16))


def _mix_kernel(h0_ref, oa_ref, u2_ref, oc_ref, wg_ref, bg_ref, woa_ref, wpw_ref, bpw_ref, woc_ref,
                wout_ref, bout_ref, g_ref, b_ref, wr_ref, br_ref, h1_ref, rl_ref):
    h0 = h0_ref[...]
    hb = h0.astype(BF16)
    dm = h0.shape[1]
    y = (_dot(oa_ref[...], woa_ref[...]),
         _dot(u2_ref[...], wpw_ref[...]) + bpw_ref[...],
         _dot(oc_ref[...], woc_ref[...]))
    mixed = jnp.zeros_like(h0)
    for k in range(N_BRANCH):
        gate = jax.nn.sigmoid(_dot(hb, wg_ref[:, k * dm:(k + 1) * dm]) + bg_ref[:, k * dm:(k + 1) * dm])
        mixed = mixed + gate * y[k]
    out = _dot(mixed.astype(BF16), wout_ref[...]) + bout_ref[...]
    h1 = _layer_norm(DN_ALPHA * h0 + out, g_ref[...], b_ref[...])
    h1_ref[...] = h1
    rl_ref[...] = _dot(h1.astype(BF16), wr_ref[...]) + br_ref[...]


def _mix(h0, oa, u2, oc, w_gate, b_gate, w_o_a, w_pw2, b_pw2, w_o_c, w_out, b_out, ln_g, ln_b, w_router, b_router,
         tm=256):
    n, dm = h0.shape
    row = lambda i: (i, 0)
    fixed = lambda i: (0, 0)
    wr = jnp.zeros((dm, LANES), F32).at[:, :N_EXPERTS].set(w_router).astype(BF16)
    br = jnp.zeros((1, LANES), F32).at[0, :N_EXPERTS].set(b_router)
    full = lambda a: pl.BlockSpec(a.shape, fixed)
    ws = [w_gate.astype(BF16), b_gate.reshape(1, -1), w_o_a.astype(BF16), w_pw2.astype(BF16), b_pw2.reshape(1, -1),
          w_o_c.astype(BF16), w_out.astype(BF16), b_out.reshape(1, -1), ln_g.reshape(1, -1), ln_b.reshape(1, -1),
          wr, br]
    return pl.pallas_call(
        _mix_kernel,
        grid=(n // tm,),
        in_specs=[pl.BlockSpec((tm, dm), row), pl.BlockSpec((tm, 512), row), pl.BlockSpec((tm, 512), row),
                  pl.BlockSpec((tm, 512), row)] + [full(a) for a in ws],
        out_specs=[pl.BlockSpec((tm, dm), row), pl.BlockSpec((tm, LANES), row)],
        out_shape=[jax.ShapeDtypeStruct((n, dm), F32), jax.ShapeDtypeStruct((n, LANES), F32)],
        compiler_params=_cparams(("arbitrary",)),
        name="mix",
    )(h0, oa, u2, oc, *ws)


def _route_kernel(rl_ref, eid_ref, gate_ref, rank_ref, cnt_ref, carry_ref):
    @pl.when(pl.program_id(0) == 0)
    def _():
        carry_ref[...] = jnp.zeros_like(carry_ref)

    tm = rl_ref.shape[0]
    lane = lax.broadcasted_iota(I32, (tm, LANES), 1)
    vals = jnp.where(lane < N_EXPERTS, rl_ref[...], -jnp.inf)
    top_v, top_i, sels = [], [], []
    for _ in range(TOP_K):
        m = jnp.max(vals, axis=1, keepdims=True)
        idx = jnp.min(jnp.where(vals == m, lane, LANES), axis=1, keepdims=True)
        sel = lane == idx
        top_v.append(m)
        top_i.append(idx)
        sels.append(sel)
        vals = jnp.where(sel, -jnp.inf, vals)
    es = [jnp.exp(v - top_v[0]) for v in top_v]
    denom = es[0] + es[1] + es[2] + es[3]
    onehot = jnp.zeros((tm, LANES), F32)
    for sel in sels:
        onehot = onehot + jnp.where(sel, 1.0, 0.0)
    r = lax.broadcasted_iota(I32, (tm, tm), 0)
    c = lax.broadcasted_iota(I32, (tm, tm), 1)
    tri = jnp.where(c < r, 1.0, 0.0).astype(BF16)
    before = _dot(tri, onehot.astype(BF16)) + carry_ref[...]
    carry = carry_ref[...] + jnp.sum(onehot, axis=0, keepdims=True)
    carry_ref[...] = carry
    cnt_ref[...] = carry
    eid = jnp.zeros((tm, LANES), I32)
    gate = jnp.zeros((tm, LANES), F32)
    rank = jnp.zeros((tm, LANES), F32)
    for k in range(TOP_K):
        eid = jnp.where(lane == k, top_i[k], eid)
        gate = jnp.where(lane == k, es[k] / denom, gate)
        rank = jnp.where(lane == k, jnp.sum(jnp.where(sels[k], before, 0.0), axis=1, keepdims=True), rank)
    eid_ref[...] = jnp.transpose(eid)[:ROUTE_ROWS]
    gate_ref[...] = gate
    rank_ref[...] = jnp.transpose(rank.astype(I32))[:ROUTE_ROWS]


ROUTE_ROWS = SUBLANES


def _route(rl, tm=512):
    n = rl.shape[0]
    row = lambda i: (i, 0)
    col = lambda i: (0, i)
    return pl.pallas_call(
        _route_kernel,
        grid=(n // tm,),
        in_specs=[pl.BlockSpec((tm, LANES), row)],
        out_specs=[pl.BlockSpec((ROUTE_ROWS, tm), col), pl.BlockSpec((tm, LANES), row),
                   pl.BlockSpec((ROUTE_ROWS, tm), col), pl.BlockSpec((1, LANES), lambda i: (0, 0))],
        out_shape=[jax.ShapeDtypeStruct((ROUTE_ROWS, n), I32), jax.ShapeDtypeStruct((n, LANES), F32),
                   jax.ShapeDtypeStruct((ROUTE_ROWS, n), I32), jax.ShapeDtypeStruct((1, LANES), F32)],
        scratch_shapes=[pltpu.VMEM((1, LANES), F32)],
        compiler_params=_cparams(("arbitrary",)),
        name="route",
    )(rl)


DMA_UNROLL = 4


def _to_tiles(x, tile_ref):
    for s in range(SUBLANES):
        tile_ref[pl.ds(s, x.shape[0], stride=SUBLANES), :] = x[:, s * LANES:(s + 1) * LANES]


def _from_tiles(tile_ref):
    rows = tile_ref.shape[0] // SUBLANES
    return jnp.concatenate([tile_ref[pl.ds(s, rows, stride=SUBLANES), :] for s in range(SUBLANES)], axis=1)


def _tile_rows(ref, row):
    return ref.at[pl.ds(pl.multiple_of(row * SUBLANES, SUBLANES), SUBLANES)]


def _dispatch_kernel(dest_ref, h1_ref, xs_in_hbm, xs_hbm, stage_ref, sem):
    del xs_in_hbm
    tm = h1_ref.shape[0]
    _to_tiles(h1_ref[...], stage_ref)

    def start(t, c):
        for k in range(TOP_K):
            pltpu.make_async_copy(_tile_rows(stage_ref, t), _tile_rows(xs_hbm, dest_ref[k, t]),
                                  sem).start(priority=k % 2)
        return c

    def wait(t, c):
        for k in range(TOP_K):
            pltpu.make_async_copy(_tile_rows(stage_ref, 0), _tile_rows(xs_hbm, 0), sem).wait()
        return c

    lax.fori_loop(0, tm, start, 0, unroll=DMA_UNROLL)
    lax.fori_loop(0, tm, wait, 0, unroll=DMA_UNROLL)


def _dispatch(h1, dest_t, n_rows, tm=256):
    n, dm = h1.shape
    xs0 = jnp.zeros((n_rows * SUBLANES, LANES), F32)
    return pl.pallas_call(
        _dispatch_kernel,
        grid=(n // tm,),
        in_specs=[pl.BlockSpec((ROUTE_ROWS, tm), lambda i: (0, i), memory_space=pltpu.SMEM),
                  pl.BlockSpec((tm, dm), lambda i: (i, 0)), pl.BlockSpec(memory_space=pl.ANY)],
        out_specs=pl.BlockSpec(memory_space=pl.ANY),
        out_shape=jax.ShapeDtypeStruct((n_rows * SUBLANES, LANES), F32),
        scratch_shapes=[pltpu.VMEM((tm * SUBLANES, LANES), F32), pltpu.SemaphoreType.DMA(())],
        input_output_aliases={2: 0},
        compiler_params=_cparams(("arbitrary",)),
        name="dispatch",
    )(dest_t, h1, xs0)


def _experts_kernel(blk_exp_ref, n_used_ref, xs_ref, w1_ref, b1_ref, w2_ref, b2_ref, ys_ref, w1b_ref, w2b_ref):
    b = pl.program_id(0)
    prev = blk_exp_ref[jnp.maximum(b - 1, 0)]
    used = b < n_used_ref[0]

    @pl.when(used & ((b == 0) | (blk_exp_ref[b] != prev)))
    def _():
        w1b_ref[...] = w1_ref[0].astype(BF16)
        w2b_ref[...] = w2_ref[0].astype(BF16)

    @pl.when(used)
    def _():
        d_exp = w2b_ref.shape[0]
        hdn = _dot(_from_tiles(xs_ref).astype(BF16), w1b_ref[...]) + b1_ref[0]
        g = jnp.minimum(hdn[:, :d_exp], SWIGLU_LIMIT)
        u = jnp.clip(hdn[:, d_exp:], -SWIGLU_LIMIT, SWIGLU_LIMIT)
        act = (u + 1.0) * (g * jax.nn.sigmoid(SWIGLU_ALPHA * g))
        _to_tiles(_dot(act.astype(BF16), w2b_ref[...]) + b2_ref[0], ys_ref)

    @pl.when(jnp.logical_not(used))
    def _():
        ys_ref[...] = jnp.zeros_like(ys_ref)


def _experts(xs, blk_exp, n_used, w1, b1, w2, b2):
    n_rows = xs.shape[0] // SUBLANES
    n_exp, dm, two_f = w1.shape
    d_exp = two_f // 2
    grid_spec = pltpu.PrefetchScalarGridSpec(
        num_scalar_prefetch=2,
        grid=(n_rows // ROW_BLOCK,),
        in_specs=[pl.BlockSpec((ROW_BLOCK * SUBLANES, LANES), lambda b, be, nu: (b, 0)),
                  pl.BlockSpec((1, dm, two_f), lambda b, be, nu: (be[b], 0, 0)),
                  pl.BlockSpec((1, 1, two_f), lambda b, be, nu: (be[b], 0, 0)),
                  pl.BlockSpec((1, d_exp, dm), lambda b, be, nu: (be[b], 0, 0)),
                  pl.BlockSpec((1, 1, dm), lambda b, be, nu: (be[b], 0, 0))],
        out_specs=pl.BlockSpec((ROW_BLOCK * SUBLANES, LANES), lambda b, be, nu: (b, 0)),
        scratch_shapes=[pltpu.VMEM((dm, two_f), BF16), pltpu.VMEM((d_exp, dm), BF16)],
    )
    return pl.pallas_call(
        _experts_kernel,
        grid_spec=grid_spec,
        out_shape=jax.ShapeDtypeStruct((n_rows * SUBLANES, LANES), F32),
        compiler_params=_cparams(("arbitrary",)),
        name="experts",
    )(blk_exp, n_used, xs, w1, b1.reshape(n_exp, 1, two_f), w2, b2.reshape(n_exp, 1, dm))


def _combine_kernel(dest_ref, ys_hbm, h1_ref, gate_ref, g_ref, b_ref, o_ref, buf_ref, sem):
    tm = h1_ref.shape[0]

    def start(t, c):
        for k in range(TOP_K):
            pltpu.make_async_copy(_tile_rows(ys_hbm, dest_ref[k, t]), _tile_rows(buf_ref.at[k], t),
                                  sem).start(priority=k % 2)
        return c

    def wait(t, c):
        for k in range(TOP_K):
            pltpu.make_async_copy(_tile_rows(ys_hbm, 0), _tile_rows(buf_ref.at[0], 0), sem).wait()
        return c

    lax.fori_loop(0, tm, start, 0, unroll=DMA_UNROLL)
    lax.fori_loop(0, tm, wait, 0, unroll=DMA_UNROLL)
    gate = gate_ref[...]
    moe = jnp.zeros(h1_ref.shape, F32)
    for k in range(TOP_K):
        moe = moe + _from_tiles(buf_ref.at[k]) * gate[:, k:k + 1]
    o_ref[...] = _layer_norm(DN_ALPHA * h1_ref[...] + moe, g_ref[...], b_ref[...])


def _combine(ys, dest_t, h1, gate, ln_g, ln_b, tm=256):
    n, dm = h1.shape
    row = lambda i: (i, 0)
    fixed = lambda i: (0, 0)
    return pl.pallas_call(
        _combine_kernel,
        grid=(n // tm,),
        in_specs=[pl.BlockSpec((ROUTE_ROWS, tm), lambda i: (0, i), memory_space=pltpu.SMEM),
                  pl.BlockSpec(memory_space=pl.ANY),
                  pl.BlockSpec((tm, dm), row), pl.BlockSpec((tm, LANES), row),
                  pl.BlockSpec((1, dm), fixed), pl.BlockSpec((1, dm), fixed)],
        out_specs=pl.BlockSpec((tm, dm), row),
        out_shape=jax.ShapeDtypeStruct((n, dm), F32),
        scratch_shapes=[pltpu.VMEM((TOP_K, tm * SUBLANES, LANES), F32), pltpu.SemaphoreType.DMA(())],
        compiler_params=_cparams(("arbitrary",)),
        name="combine",
    )(dest_t, ys, h1, gate, ln_g.reshape(1, dm), ln_b.reshape(1, dm))


def _moe(h1, rl, w1, b1, w2, b2, ln_g, ln_b):
    n, dm = h1.shape
    eid_t, gate, rank_t, counts = _route(rl)
    counts = counts[0, :N_EXPERTS].astype(I32)
    padded = (counts + ROW_BLOCK - 1) // ROW_BLOCK * ROW_BLOCK
    pad_end = jnp.cumsum(padded)
    group_start = pad_end - padded
    n_blocks = n * TOP_K // ROW_BLOCK + N_EXPERTS
    n_used = (pad_end[-1] // ROW_BLOCK).astype(I32)
    blk = jnp.minimum(jnp.arange(n_blocks, dtype=I32), n_used - 1)
    blk_exp = jnp.sum((pad_end[None, :] <= (blk * ROW_BLOCK)[:, None]).astype(I32), axis=1)
    expert = jnp.arange(N_EXPERTS, dtype=I32)[:, None, None]
    dest_t = jnp.sum(jnp.where(eid_t[None] == expert, group_start[:, None, None], 0), axis=0) + rank_t
    xs = _dispatch(h1, dest_t, n_blocks * ROW_BLOCK)
    ys = _experts(xs, blk_exp, n_used.reshape(1), w1, b1, w2, b2)
    return _combine(ys, dest_t, h1, gate, ln_g, ln_b)


def kernel(x, mem, ln0_g, ln0_b, w_in, b_in, kv_norm_g, w_uk, w_uv, w_o_a, w_dw, b_dw, conv_ln_g, conv_ln_b, w_pw2, b_pw2, w_mem_k, w_mem_v, w_o_c, w_out, b_out, ln1_g, ln1_b, w_router, b_router, w1, b1, w2, b2, ln2_g, ln2_b):
    bsz, seq, dm = x.shape
    n = bsz * seq
    assert w_in.shape[0] == DEPTH == 1, "the stem LayerNorm is fused into the single layer's projection"
    l = 0
    gate_lo = w_in.shape[-1] - N_BRANCH * dm
    h0, qa, ckv, qidx, krep, widx, u, qc = _proj(x.reshape(n, dm), ln0_g, ln0_b, w_in[l, :, :gate_lo],
                                                   b_in[l, :gate_lo], kv_norm_g[l])
    oa = _dsa_attention(qidx, widx, qa, krep, ckv, w_uk[l], w_uv[l], bsz, seq).reshape(n, -1)
    u2 = _conv(u, w_dw[l], b_dw[l], conv_ln_g[l], conv_ln_b[l], bsz, seq).reshape(n, -1)
    oc = _mem_attention(qc, mem, w_mem_k[l], w_mem_v[l], bsz, seq).reshape(n, -1)
    h1, rl = _mix(h0, oa, u2, oc, w_in[l, :, gate_lo:], b_in[l, gate_lo:], w_o_a[l], w_pw2[l], b_pw2[l],
                  w_o_c[l], w_out[l], b_out[l], ln1_g[l], ln1_b[l], w_router[l], b_router[l])
    h2 = _moe(h1, rl, w1[l], b1[l], w2[l], b2[l], ln2_g[l], ln2_b[l])
    return h2.reshape(bsz, seq, dm)
```

```python
import jax
import jax.numpy as jnp
from jax import lax
from jax.experimental import pallas as pl
from jax.experimental.pallas import tpu as pltpu

F32 = jnp.float32
BF16 = jnp.bfloat16
I32 = jnp.int32

CHUNK = 64
A_HEADS = 8
A_HEAD_DIM = 64
KV_RANK = 128
IDX_HEADS = 8
IDX_DIM = 32
TOPK_MAX = 256
CONV_WIDTH = 31
C_HEADS = 4
C_HEAD_DIM = 128
N_BRANCH = 3
N_EXPERTS = 32
TOP_K = 4
SWIGLU_LIMIT = 7.0
SWIGLU_ALPHA = 1.702
LN_EPS = 1e-5
NEG_INF = -1e30
DEPTH = 1
DN_ALPHA = (2 * DEPTH) ** 0.25
A_SCALE = A_HEAD_DIM ** -0.5
IDX_SCALE = (IDX_HEADS * IDX_DIM) ** -0.5
C_SCALE = C_HEAD_DIM ** -0.5

LANES = 128
SUBLANES = 8
Q_BLOCK = 128
KEY_TILE = 512
ATTEND_UNROLLS = (4, 2, 1)
COARSE_BITS = 20
FINE_KEEP = 4
FINE_CHAINS = 4
ROW_BLOCK = 512
VMEM_LIMIT = 56 * 1024 * 1024


def _cparams(sem, vmem=VMEM_LIMIT):
    return pltpu.CompilerParams(dimension_semantics=sem, vmem_limit_bytes=vmem)


def _layer_norm(x, g, b):
    xc = x - jnp.mean(x, -1, keepdims=True)
    var = jnp.mean(xc * xc, -1, keepdims=True)
    return xc * lax.rsqrt(var + LN_EPS) * g + b


def _dot(a, b):
    return jnp.dot(a, b, preferred_element_type=F32)


def _dot_nt(a, b):
    return lax.dot_general(a, b, (((1,), (1,)), ((), ())), preferred_element_type=F32)


_P_QA = (0, 512)
_P_CKV = (512, 640)
_P_QIDX = (640, 896)
_P_KREP = (896, 1152)
_P_WIDX = (1152, 1280)
_P_GLU_A = (1280, 1792)
_P_GLU_G = (1792, 2304)
_P_QC = (2304, 2816)
_P_COLS = 2816


def _proj_kernel(x_ref, g_ref, b_ref, w_ref, bias_ref, kvg_ref,
                 h0_ref, qa_ref, ckv_ref, qidx_ref, krep_ref, widx_ref, u_ref, qc_ref):
    h = _layer_norm(x_ref[...], g_ref[...], b_ref[...])
    h0_ref[...] = h
    hb = h.astype(BF16)

    def proj(span):
        lo, hi = span
        return _dot(hb, w_ref[:, lo:hi]) + bias_ref[:, lo:hi]

    qa_ref[...] = proj(_P_QA).astype(BF16)
    ckv = proj(_P_CKV)
    ckv = ckv * lax.rsqrt(jnp.mean(ckv * ckv, -1, keepdims=True) + LN_EPS) * kvg_ref[...]
    ckv_ref[...] = ckv.astype(BF16)
    qidx_ref[...] = proj(_P_QIDX).astype(BF16)
    krep_ref[...] = proj(_P_KREP).astype(BF16)
    widx_ref[...] = proj(_P_WIDX) * IDX_SCALE
    u_ref[...] = proj(_P_GLU_A) * jax.nn.sigmoid(proj(_P_GLU_G))
    qc_ref[...] = proj(_P_QC).astype(BF16)


def _proj(xt, ln_g, ln_b, w_in, b_in, kv_g, tm=512):
    n, dm = xt.shape
    o = 0
    parts = {}
    for name, width in (("qa", 512), ("ckv", 128), ("qidx", 256), ("kidx", 32), ("widx", 8),
                        ("glu", 1024), ("qc", 512)):
        parts[name] = (o, o + width)
        o += width

    def cols(a, name):
        lo, hi = parts[name]
        return a[..., lo:hi]

    def pack(a):
        pad = jnp.zeros(a.shape[:-1] + (LANES - 8,), a.dtype)
        glu = cols(a, "glu")
        return jnp.concatenate(
            [cols(a, "qa"), cols(a, "ckv"), cols(a, "qidx"),
             jnp.concatenate([cols(a, "kidx")] * IDX_HEADS, -1),
             cols(a, "widx"), pad, glu[..., :512], glu[..., 512:], cols(a, "qc")], -1)

    w = pack(w_in).astype(BF16)
    bias = pack(b_in).reshape(1, _P_COLS)
    row = lambda i: (i, 0)
    fixed = lambda i: (0, 0)
    outs = [(dm, F32), (512, BF16), (128, BF16), (256, BF16), (256, BF16), (128, F32), (512, F32), (512, BF16)]
    return pl.pallas_call(
        _proj_kernel,
        grid=(n // tm,),
        in_specs=[pl.BlockSpec((tm, dm), row), pl.BlockSpec((1, dm), fixed), pl.BlockSpec((1, dm), fixed),
                  pl.BlockSpec((dm, _P_COLS), fixed), pl.BlockSpec((1, _P_COLS), fixed),
                  pl.BlockSpec((1, KV_RANK), fixed)],
        out_specs=[pl.BlockSpec((tm, c), row) for c, _ in outs],
        out_shape=[jax.ShapeDtypeStruct((n, c), d) for c, d in outs],
        compiler_params=_cparams(("arbitrary",)),
        name="proj",
    )(xt, ln_g.reshape(1, dm), ln_b.reshape(1, dm), w, bias, kv_g.reshape(1, KV_RANK))


def _alibi_slope(h):
    return 2.0 ** (-(8.0 / A_HEADS) * (h + 1))


def _attn_kernel(qidx_ref, widx_ref, qa_ref, krep_ref, ckv_ref, ext_ref, wuk_ref, wuv_ref, o_ref,
                 kp_ref, sc_ref, p_ref, acc_ref, m_ref):
    i = pl.program_id(1)
    t0 = i * Q_BLOCK

    @pl.when(i == 0)
    def _():
        kp_ref[:, :KV_RANK] = ckv_ref[0]
        kp_ref[:, KV_RANK:] = ext_ref[...]

    n_full = i // (KEY_TILE // Q_BLOCK)
    n_tiles = n_full + 1

    def key_rows(ref, j):
        return ref[pl.ds(pl.multiple_of(j * KEY_TILE, KEY_TILE), KEY_TILE), :]

    qi = qidx_ref[0]
    head_of_lane = lax.broadcasted_iota(I32, (1, IDX_HEADS * IDX_DIM), 1) // IDX_DIM
    qbd = jnp.concatenate([jnp.where(head_of_lane == h, qi, jnp.zeros_like(qi)) for h in range(IDX_HEADS)], 0)
    w_t = jnp.transpose(widx_ref[0])

    def score_tile(j):
        s = _dot_nt(key_rows(krep_ref.at[0], j), qbd)
        acc = jnp.zeros((KEY_TILE, Q_BLOCK), F32)
        for h in range(IDX_HEADS):
            acc = acc + jnp.maximum(s[:, h * Q_BLOCK:(h + 1) * Q_BLOCK], 0.0) * w_t[h:h + 1, :]
        return acc + 0.0

    def score_pair(jj, c):
        sc_ref[2 * jj] = score_tile(2 * jj)
        sc_ref[2 * jj + 1] = score_tile(2 * jj + 1)
        return c

    def score_single(j, c):
        sc_ref[j] = score_tile(j)
        return c

    lax.fori_loop(0, n_full // 2, score_pair, 0)
    lax.fori_loop(n_full // 2 * 2, n_full, score_single, 0)
    key_row = lax.broadcasted_iota(I32, (KEY_TILE, Q_BLOCK), 0)
    q_pos = t0 + lax.broadcasted_iota(I32, (KEY_TILE, Q_BLOCK), 1)
    key_last = n_full * KEY_TILE + key_row
    adm_last = (key_last // CHUNK) <= (q_pos // CHUNK)
    sc_ref[n_full] = jnp.where(adm_last, score_tile(n_full), -jnp.inf)

    def by_sublane(a):
        return a.reshape(KEY_TILE // SUBLANES, SUBLANES, Q_BLOCK)

    def scan_tiles(step, carry):
        carry = lax.fori_loop(0, n_tiles // 2, lambda jj, c: step(2 * jj + 1, step(2 * jj, c)), carry)
        return lax.fori_loop(n_tiles // 2 * 2, n_tiles, step, carry)

    def count_where(*preds):
        def step(j, cnts):
            s = sc_ref[j]
            return tuple(cnt + jnp.sum(by_sublane(jnp.where(pred(s, j * KEY_TILE), 1, 0)), axis=0)
                         for cnt, pred in zip(cnts, preds))
        cnts = scan_tiles(step, (jnp.zeros((SUBLANES, Q_BLOCK), I32),) * len(preds))
        cnts = [jnp.sum(c, axis=0, keepdims=True) for c in cnts]
        return cnts[0] if len(preds) == 1 else cnts

    def key_to_float(key):
        bits = key ^ ((key >> 31) & jnp.int32(0x7FFFFFFF))
        return lax.bitcast_convert_type(bits, F32)

    int_min = jnp.int32(-2 ** 31)

    def bit_step(b, prefix):
        cand = prefix + lax.shift_left(jnp.int32(1), jnp.int32(31) - b)
        cand_f = key_to_float(cand)
        return jnp.where(count_where(lambda s, base: s >= cand_f) >= TOPK_MAX, cand, prefix)

    coarse = lax.fori_loop(0, COARSE_BITS, bit_step, jnp.full((1, Q_BLOCK), int_min, I32))
    enough = coarse != int_min

    top_f = key_to_float(coarse + jnp.int32(1 << (32 - COARSE_BITS)))

    def insert(best, v):
        out = []
        for b in best:
            out.append(jnp.maximum(b, v))
            v = jnp.minimum(b, v)
        return out

    def merge(best, other):
        for v in other:
            best = insert(best, v)
        return best

    def fine_step(j, carry):
        cnt, lists = carry[0], [list(carry[1 + FINE_KEEP * c:1 + FINE_KEEP * (c + 1)]) for c in range(FINE_CHAINS)]
        s = sc_ref[j]
        above = s >= top_f
        cnt = cnt + jnp.sum(by_sublane(jnp.where(above, 1, 0)), axis=0)
        below = by_sublane(jnp.where(above, -jnp.inf, s))
        for g in range(KEY_TILE // SUBLANES):
            lists[g % FINE_CHAINS] = insert(lists[g % FINE_CHAINS], below[g])
        return (cnt,) + tuple(v for lst in lists for v in lst)

    lowest = jnp.full((SUBLANES, Q_BLOCK), -jnp.inf, F32)
    carry = scan_tiles(fine_step, (jnp.zeros((SUBLANES, Q_BLOCK), I32),) + (lowest,) * (FINE_KEEP * FINE_CHAINS))
    n_above = jnp.sum(carry[0], axis=0, keepdims=True)
    best = list(carry[1:1 + FINE_KEEP])
    for c in range(1, FINE_CHAINS):
        best = merge(best, carry[1 + FINE_KEEP * c:1 + FINE_KEEP * (c + 1)])
    top = [b[0:1] for b in best]
    for sub in range(1, SUBLANES):
        top = merge(top, [b[sub:sub + 1] for b in best])
    r = TOPK_MAX - n_above
    thr_fast = top[FINE_KEEP - 1]
    for k in range(FINE_KEEP - 1, 0, -1):
        thr_fast = jnp.where(r == k, top[k - 1], thr_fast)
    found = (r <= FINE_KEEP) | jnp.logical_not(enough)
    resolved = jnp.min(jnp.where(found, 1, 0)) > 0

    def remaining_bits(_):
        return key_to_float(lax.fori_loop(COARSE_BITS, 32, bit_step, coarse))

    def many_in_range(_):
        bottom = key_to_float(coarse)
        at_bottom = found | (count_where(lambda s, base: s > bottom) < TOPK_MAX)
        value = jnp.where(found, thr_fast, bottom)
        return lax.cond(jnp.min(jnp.where(at_bottom, 1, 0)) > 0, lambda _: value, remaining_bits, 0)

    thr = lax.cond(resolved, lambda _: thr_fast, many_in_range, 0)
    thr = jnp.where(enough, thr, -jnp.inf)

    n_gt, n_eq = count_where(lambda s, base: s > thr, lambda s, base: s == thr)
    need = TOPK_MAX - n_gt
    seq_bits = 13

    def tie_search(_):
        def step(b, lo):
            cand = lo + lax.shift_left(jnp.int32(1), jnp.int32(seq_bits) - b)
            below = count_where(lambda s, base: (s == thr) & ((base + key_row) < cand))
            return jnp.where(below < need, cand, lo)
        return lax.fori_loop(0, seq_bits + 1, step, jnp.zeros((1, Q_BLOCK), I32))

    overflow = jnp.max(jnp.where(n_gt + n_eq > TOPK_MAX, 1, 0)) > 0
    last_tie = lax.cond(overflow, tie_search, lambda _: jnp.full((1, Q_BLOCK), 2 ** 30, I32), 0)

    def mask_bias(j):
        s = sc_ref[j]
        tie_ok = jnp.where((j * KEY_TILE + key_row) <= last_tie, 0.0, NEG_INF)
        mb = jnp.where(s > thr, 0.0, jnp.where(s == thr, tie_ok, NEG_INF))
        return jnp.transpose(mb)

    qlat = _dot(qa_ref[0], wuk_ref[...]) * A_SCALE
    lane_e = lax.broadcasted_iota(I32, (Q_BLOCK, LANES), 1)
    t0f = t0.astype(F32)
    qp = []
    for h in range(A_HEADS):
        sl = _alibi_slope(h)
        qext = jnp.where(lane_e == 0, CHUNK * sl, jnp.where(lane_e == 1, sl, jnp.where(lane_e == 2, -sl * t0f, 0.0)))
        qp.append(jnp.concatenate([qlat[:, h * KV_RANK:(h + 1) * KV_RANK], qext], 1))
    qp = jnp.concatenate(qp, 0).astype(BF16)

    m_ref[...] = jnp.full(m_ref.shape, NEG_INF, F32)
    acc_ref[...] = jnp.zeros(acc_ref.shape, F32)

    def attend(j, slot, last=False):
        kt = key_rows(kp_ref, j)
        logits = _dot_nt(qp, kt)
        mb = mask_bias(j)
        if last:
            row_t = t0 + lax.broadcasted_iota(I32, (Q_BLOCK, KEY_TILE), 0)
            key_s = n_full * KEY_TILE + lax.broadcasted_iota(I32, (Q_BLOCK, KEY_TILE), 1)
            mb = jnp.where((key_s // CHUNK) <= (row_t // CHUNK), mb, NEG_INF)
            ahead = jnp.maximum(key_s - row_t, 0).astype(F32)
        alphas = []
        for h in range(A_HEADS):
            l = logits[h * Q_BLOCK:(h + 1) * Q_BLOCK] + mb
            if last:
                l = l - (2.0 * _alibi_slope(h)) * ahead
            m_old = m_ref[h]
            m_new = jnp.maximum(m_old, jnp.broadcast_to(jnp.max(l, axis=1, keepdims=True), (Q_BLOCK, LANES)))
            m_ref[h] = m_new
            alphas.append(jnp.exp(m_old - m_new))
            p = jnp.exp(l - jnp.concatenate([m_new] * (KEY_TILE // LANES), 1))
            p_ref[slot, h * Q_BLOCK:(h + 1) * Q_BLOCK, :] = p.astype(BF16)
        pv = _dot(p_ref[slot], kt)
        for h in range(A_HEADS):
            acc_ref[h] = acc_ref[h] * jnp.concatenate([alphas[h]] * 2, 1) + pv[h * Q_BLOCK:(h + 1) * Q_BLOCK]

    def attend_group(width, first):
        def body(jj, c):
            for k in range(width):
                attend(first + width * jj + k, k % 2)
            return c
        return body

    done = 0
    for width in ATTEND_UNROLLS:
        trips = (n_full - done) // width
        lax.fori_loop(0, trips, attend_group(width, done), 0)
        done = done + trips * width
    attend(n_full, 0, last=True)

    outs = []
    for h in range(A_HEADS):
        a = acc_ref[h]
        outs.append(a[:, :KV_RANK] / a[:, KV_RANK + 2:KV_RANK + 3])
    olat = jnp.concatenate(outs, 1).astype(BF16)
    o_ref[0] = _dot(olat, wuv_ref[...]).astype(BF16)


def _dsa_attention(qidx, widx, qa, krep, ckv, w_uk, w_uv, bsz, seq):
    nq = seq // Q_BLOCK
    n_key_tiles = seq // KEY_TILE
    pos = lax.broadcasted_iota(I32, (seq, LANES), 0)
    lane = lax.broadcasted_iota(I32, (seq, LANES), 1)
    ext = jnp.where(lane == 0, pos // CHUNK, jnp.where(lane == 1, pos % CHUNK, jnp.where(lane == 2, 1, 0)))
    ext = ext.astype(BF16)
    eye = jnp.eye(A_HEADS, dtype=F32)
    wuk_bd = jnp.einsum("hg,hrd->hdgr", eye, w_uk).reshape(A_HEADS * A_HEAD_DIM, A_HEADS * KV_RANK)
    wuv_bd = jnp.einsum("hg,hrd->hrgd", eye, w_uv).reshape(A_HEADS * KV_RANK, A_HEADS * A_HEAD_DIM)
    blk = lambda b, i: (b, i, 0)
    whole = lambda b, i: (b, 0, 0)
    fixed = lambda b, i: (0, 0)
    r3 = lambda a: a.reshape(bsz, seq, a.shape[-1])
    return pl.pallas_call(
        _attn_kernel,
        grid=(bsz, nq),
        in_specs=[pl.BlockSpec((1, Q_BLOCK, 256), blk), pl.BlockSpec((1, Q_BLOCK, LANES), blk),
                  pl.BlockSpec((1, Q_BLOCK, 512), blk),
                  pl.BlockSpec((1, seq, 256), whole), pl.BlockSpec((1, seq, KV_RANK), whole),
                  pl.BlockSpec((seq, LANES), fixed),
                  pl.BlockSpec((512, 1024), fixed), pl.BlockSpec((1024, 512), fixed)],
        out_specs=pl.BlockSpec((1, Q_BLOCK, 512), blk),
        out_shape=jax.ShapeDtypeStruct((bsz, seq, 512), BF16),
        scratch_shapes=[pltpu.VMEM((seq, 2 * KV_RANK), BF16),
                        pltpu.VMEM((n_key_tiles, KEY_TILE, Q_BLOCK), F32),
                        pltpu.VMEM((2, A_HEADS * Q_BLOCK, KEY_TILE), BF16),
                        pltpu.VMEM((A_HEADS, Q_BLOCK, 2 * KV_RANK), F32),
                        pltpu.VMEM((A_HEADS, Q_BLOCK, LANES), F32)],
        compiler_params=_cparams(("arbitrary", "arbitrary")),
        name="attn",
    )(r3(qidx), r3(widx), r3(qa), r3(krep), r3(ckv), ext, wuk_bd.astype(BF16), wuv_bd.astype(BF16))


CONV_HALO = 32
CONV_PHASES = SUBLANES


def _conv_kernel(cur_ref, prev_ref, w_ref, b_ref, g_ref, beta_ref, o_ref, win_ref, shift_ref):
    j = pl.program_id(1)
    tq = cur_ref.shape[1]
    halo = prev_ref[0, tq - CONV_HALO:, :]
    win_ref[:CONV_HALO] = jnp.where(j > 0, halo, jnp.zeros_like(halo))
    win_ref[CONV_HALO:] = cur_ref[0]
    acc = jnp.zeros((tq, cur_ref.shape[2]), F32)
    first = CONV_HALO - (CONV_WIDTH - 1)
    for r in range(CONV_PHASES):
        taps = range(r, CONV_WIDTH, CONV_PHASES)
        rows = taps[-1] - r + tq
        shift_ref[r, :rows] = win_ref[first + r:first + r + rows, :]
        for k in taps:
            acc = acc + shift_ref[r, k - r:k - r + tq] * w_ref[k:k + 1, :]
    y = _layer_norm(acc + b_ref[...], g_ref[...], beta_ref[...])
    o_ref[0] = (y * jax.nn.sigmoid(y)).astype(BF16)


def _conv(u, w_dw, b_dw, ln_g, ln_b, bsz, seq, tq=512):
    c = u.shape[-1]
    u3 = u.reshape(bsz, seq, c)
    fixed = lambda b, j: (0, 0)
    return pl.pallas_call(
        _conv_kernel,
        grid=(bsz, seq // tq),
        in_specs=[pl.BlockSpec((1, tq, c), lambda b, j: (b, j, 0)),
                  pl.BlockSpec((1, tq, c), lambda b, j: (b, jnp.maximum(j - 1, 0), 0)),
                  pl.BlockSpec((CONV_WIDTH, c), fixed), pl.BlockSpec((1, c), fixed),
                  pl.BlockSpec((1, c), fixed), pl.BlockSpec((1, c), fixed)],
        out_specs=pl.BlockSpec((1, tq, c), lambda b, j: (b, j, 0)),
        out_shape=jax.ShapeDtypeStruct((bsz, seq, c), BF16),
        scratch_shapes=[pltpu.VMEM((tq + CONV_HALO, c), F32),
                        pltpu.VMEM((CONV_PHASES, tq + CONV_HALO, c), F32)],
        compiler_params=_cparams(("arbitrary", "arbitrary")),
        name="conv",
    )(u3, u3, w_dw, b_dw.reshape(1, c), ln_g.reshape(1, c), ln_b.reshape(1, c))


def _memattn_kernel(q_ref, mem_ref, wk_ref, wv_ref, o_ref, k_ref, v_ref):
    @pl.when(pl.program_id(1) == 0)
    def _():
        mb = mem_ref[0].astype(BF16)
        k_ref[...] = _dot(mb, wk_ref[...]).astype(BF16)
        v_ref[...] = _dot(mb, wv_ref[...]).astype(BF16)

    q = q_ref[0]
    outs = []
    for h in range(C_HEADS):
        sl = slice(h * C_HEAD_DIM, (h + 1) * C_HEAD_DIM)
        s = _dot_nt(q[:, sl], k_ref[:, sl]) * C_SCALE
        e = jnp.exp(s - jnp.max(s, axis=1, keepdims=True))
        p = e / jnp.sum(e, axis=1, keepdims=True)
        outs.append(_dot(p.astype(BF16), v_ref[:, sl]))
    o_ref[0] = jnp.concatenate(outs, 1).astype(BF16)


def _mem_attention(qc, mem, w_mem_k, w_mem_v, bsz, seq, tq=512):
    n_mem, dm = mem.shape[1:]
    cw = C_HEADS * C_HEAD_DIM
    fixed = lambda b, j: (0, 0)
    return pl.pallas_call(
        _memattn_kernel,
        grid=(bsz, seq // tq),
        in_specs=[pl.BlockSpec((1, tq, cw), lambda b, j: (b, j, 0)),
                  pl.BlockSpec((1, n_mem, dm), lambda b, j: (b, 0, 0)),
                  pl.BlockSpec((dm, cw), fixed), pl.BlockSpec((dm, cw), fixed)],
        out_specs=pl.BlockSpec((1, tq, cw), lambda b, j: (b, j, 0)),
        out_shape=jax.ShapeDtypeStruct((bsz, seq, cw), BF16),
        scratch_shapes=[pltpu.VMEM((n_mem, cw), BF16), pltpu.VMEM((n_mem, cw), BF16)],
        compiler_params=_cparams(("arbitrary", "arbitrary")),
        name="memattn",
    )(qc.reshape(bsz, seq, cw), mem, w_mem_k.astype(BF16), w_mem_v.astype(BF16))


def _mix_kernel(h0_ref, oa_ref, u2_ref, oc_ref, wg_ref, bg_ref, woa_ref, wpw_ref, bpw_ref, woc_ref,
                wout_ref, bout_ref, g_ref, b_ref, wr_ref, br_ref, h1_ref, rl_ref):
    h0 = h0_ref[...]
    hb = h0.astype(BF16)
    dm = h0.shape[1]
    y = (_dot(oa_ref[...], woa_ref[...]),
         _dot(u2_ref[...], wpw_ref[...]) + bpw_ref[...],
         _dot(oc_ref[...], woc_ref[...]))
    mixed = jnp.zeros_like(h0)
    for k in range(N_BRANCH):
        gate = jax.nn.sigmoid(_dot(hb, wg_ref[:, k * dm:(k + 1) * dm]) + bg_ref[:, k * dm:(k + 1) * dm])
        mixed = mixed + gate * y[k]
    out = _dot(mixed.astype(BF16), wout_ref[...]) + bout_ref[...]
    h1 = _layer_norm(DN_ALPHA * h0 + out, g_ref[...], b_ref[...])
    h1_ref[...] = h1
    rl_ref[...] = _dot(h1.astype(BF16), wr_ref[...]) + br_ref[...]


def _mix(h0, oa, u2, oc, w_gate, b_gate, w_o_a, w_pw2, b_pw2, w_o_c, w_out, b_out, ln_g, ln_b, w_router, b_router,
         tm=256):
    n, dm = h0.shape
    row = lambda i: (i, 0)
    fixed = lambda i: (0, 0)
    wr = jnp.zeros((dm, LANES), F32).at[:, :N_EXPERTS].set(w_router).astype(BF16)
    br = jnp.zeros((1, LANES), F32).at[0, :N_EXPERTS].set(b_router)
    full = lambda a: pl.BlockSpec(a.shape, fixed)
    ws = [w_gate.astype(BF16), b_gate.reshape(1, -1), w_o_a.astype(BF16), w_pw2.astype(BF16), b_pw2.reshape(1, -1),
          w_o_c.astype(BF16), w_out.astype(BF16), b_out.reshape(1, -1), ln_g.reshape(1, -1), ln_b.reshape(1, -1),
          wr, br]
    return pl.pallas_call(
        _mix_kernel,
        grid=(n // tm,),
        in_specs=[pl.BlockSpec((tm, dm), row), pl.BlockSpec((tm, 512), row), pl.BlockSpec((tm, 512), row),
                  pl.BlockSpec((tm, 512), row)] + [full(a) for a in ws],
        out_specs=[pl.BlockSpec((tm, dm), row), pl.BlockSpec((tm, LANES), row)],
        out_shape=[jax.ShapeDtypeStruct((n, dm), F32), jax.ShapeDtypeStruct((n, LANES), F32)],
        compiler_params=_cparams(("arbitrary",)),
        name="mix",
    )(h0, oa, u2, oc, *ws)


def _route_kernel(rl_ref, eid_ref, gate_ref, rank_ref, cnt_ref, carry_ref):
    @pl.when(pl.program_id(0) == 0)
    def _():
        carry_ref[...] = jnp.zeros_like(carry_ref)

    tm = rl_ref.shape[0]
    lane = lax.broadcasted_iota(I32, (tm, LANES), 1)
    vals = jnp.where(lane < N_EXPERTS, rl_ref[...], -jnp.inf)
    top_v, top_i, sels = [], [], []
    for _ in range(TOP_K):
        m = jnp.max(vals, axis=1, keepdims=True)
        idx = jnp.min(jnp.where(vals == m, lane, LANES), axis=1, keepdims=True)
        sel = lane == idx
        top_v.append(m)
        top_i.append(idx)
        sels.append(sel)
        vals = jnp.where(sel, -jnp.inf, vals)
    es = [jnp.exp(v - top_v[0]) for v in top_v]
    denom = es[0] + es[1] + es[2] + es[3]
    onehot = jnp.zeros((tm, LANES), F32)
    for sel in sels:
        onehot = onehot + jnp.where(sel, 1.0, 0.0)
    r = lax.broadcasted_iota(I32, (tm, tm), 0)
    c = lax.broadcasted_iota(I32, (tm, tm), 1)
    tri = jnp.where(c < r, 1.0, 0.0).astype(BF16)
    before = _dot(tri, onehot.astype(BF16)) + carry_ref[...]
    carry = carry_ref[...] + jnp.sum(onehot, axis=0, keepdims=True)
    carry_ref[...] = carry
    cnt_ref[...] = carry
    eid = jnp.zeros((tm, LANES), I32)
    gate = jnp.zeros((tm, LANES), F32)
    rank = jnp.zeros((tm, LANES), F32)
    for k in range(TOP_K):
        eid = jnp.where(lane == k, top_i[k], eid)
        gate = jnp.where(lane == k, es[k] / denom, gate)
        rank = jnp.where(lane == k, jnp.sum(jnp.where(sels[k], before, 0.0), axis=1, keepdims=True), rank)
    eid_ref[...] = jnp.transpose(eid)[:ROUTE_ROWS]
    gate_ref[...] = gate
    rank_ref[...] = jnp.transpose(rank.astype(I32))[:ROUTE_ROWS]


ROUTE_ROWS = SUBLANES


def _route(rl, tm=512):
    n = rl.shape[0]
    row = lambda i: (i, 0)
    col = lambda i: (0, i)
    return pl.pallas_call(
        _route_kernel,
        grid=(n // tm,),
        in_specs=[pl.BlockSpec((tm, LANES), row)],
        out_specs=[pl.BlockSpec((ROUTE_ROWS, tm), col), pl.BlockSpec((tm, LANES), row),
                   pl.BlockSpec((ROUTE_ROWS, tm), col), pl.BlockSpec((1, LANES), lambda i: (0, 0))],
        out_shape=[jax.ShapeDtypeStruct((ROUTE_ROWS, n), I32), jax.ShapeDtypeStruct((n, LANES), F32),
                   jax.ShapeDtypeStruct((ROUTE_ROWS, n), I32), jax.ShapeDtypeStruct((1, LANES), F32)],
        scratch_shapes=[pltpu.VMEM((1, LANES), F32)],
        compiler_params=_cparams(("arbitrary",)),
        name="route",
    )(rl)


DMA_UNROLL = 4


def _to_tiles(x, tile_ref):
    for s in range(SUBLANES):
        tile_ref[pl.ds(s, x.shape[0], stride=SUBLANES), :] = x[:, s * LANES:(s + 1) * LANES]


def _from_tiles(tile_ref):
    rows = tile_ref.shape[0] // SUBLANES
    return jnp.concatenate([tile_ref[pl.ds(s, rows, stride=SUBLANES), :] for s in range(SUBLANES)], axis=1)


def _tile_rows(ref, row):
    return ref.at[pl.ds(pl.multiple_of(row * SUBLANES, SUBLANES), SUBLANES)]


def _dispatch_kernel(dest_ref, h1_ref, xs_in_hbm, xs_hbm, stage_ref, sem):
    del xs_in_hbm
    tm = h1_ref.shape[0]
    _to_tiles(h1_ref[...], stage_ref)

    def start(t, c):
        for k in range(TOP_K):
            pltpu.make_async_copy(_tile_rows(stage_ref, t), _tile_rows(xs_hbm, dest_ref[k, t]),
                                  sem).start(priority=k % 2)
        return c

    def wait(t, c):
        for k in range(TOP_K):
            pltpu.make_async_copy(_tile_rows(stage_ref, 0), _tile_rows(xs_hbm, 0), sem).wait()
        return c

    lax.fori_loop(0, tm, start, 0, unroll=DMA_UNROLL)
    lax.fori_loop(0, tm, wait, 0, unroll=DMA_UNROLL)


def _dispatch(h1, dest_t, n_rows, tm=256):
    n, dm = h1.shape
    xs0 = jnp.zeros((n_rows * SUBLANES, LANES), F32)
    return pl.pallas_call(
        _dispatch_kernel,
        grid=(n // tm,),
        in_specs=[pl.BlockSpec((ROUTE_ROWS, tm), lambda i: (0, i), memory_space=pltpu.SMEM),
                  pl.BlockSpec((tm, dm), lambda i: (i, 0)), pl.BlockSpec(memory_space=pl.ANY)],
        out_specs=pl.BlockSpec(memory_space=pl.ANY),
        out_shape=jax.ShapeDtypeStruct((n_rows * SUBLANES, LANES), F32),
        scratch_shapes=[pltpu.VMEM((tm * SUBLANES, LANES), F32), pltpu.SemaphoreType.DMA(())],
        input_output_aliases={2: 0},
        compiler_params=_cparams(("arbitrary",)),
        name="dispatch",
    )(dest_t, h1, xs0)


def _experts_kernel(blk_exp_ref, n_used_ref, xs_ref, w1_ref, b1_ref, w2_ref, b2_ref, ys_ref, w1b_ref, w2b_ref):
    b = pl.program_id(0)
    prev = blk_exp_ref[jnp.maximum(b - 1, 0)]
    used = b < n_used_ref[0]

    @pl.when(used & ((b == 0) | (blk_exp_ref[b] != prev)))
    def _():
        w1b_ref[...] = w1_ref[0].astype(BF16)
        w2b_ref[...] = w2_ref[0].astype(BF16)

    @pl.when(used)
    def _():
        d_exp = w2b_ref.shape[0]
        hdn = _dot(_from_tiles(xs_ref).astype(BF16), w1b_ref[...]) + b1_ref[0]
        g = jnp.minimum(hdn[:, :d_exp], SWIGLU_LIMIT)
        u = jnp.clip(hdn[:, d_exp:], -SWIGLU_LIMIT, SWIGLU_LIMIT)
        act = (u + 1.0) * (g * jax.nn.sigmoid(SWIGLU_ALPHA * g))
        _to_tiles(_dot(act.astype(BF16), w2b_ref[...]) + b2_ref[0], ys_ref)

    @pl.when(jnp.logical_not(used))
    def _():
        ys_ref[...] = jnp.zeros_like(ys_ref)


def _experts(xs, blk_exp, n_used, w1, b1, w2, b2):
    n_rows = xs.shape[0] // SUBLANES
    n_exp, dm, two_f = w1.shape
    d_exp = two_f // 2
    grid_spec = pltpu.PrefetchScalarGridSpec(
        num_scalar_prefetch=2,
        grid=(n_rows // ROW_BLOCK,),
        in_specs=[pl.BlockSpec((ROW_BLOCK * SUBLANES, LANES), lambda b, be, nu: (b, 0)),
                  pl.BlockSpec((1, dm, two_f), lambda b, be, nu: (be[b], 0, 0)),
                  pl.BlockSpec((1, 1, two_f), lambda b, be, nu: (be[b], 0, 0)),
                  pl.BlockSpec((1, d_exp, dm), lambda b, be, nu: (be[b], 0, 0)),
                  pl.BlockSpec((1, 1, dm), lambda b, be, nu: (be[b], 0, 0))],
        out_specs=pl.BlockSpec((ROW_BLOCK * SUBLANES, LANES), lambda b, be, nu: (b, 0)),
        scratch_shapes=[pltpu.VMEM((dm, two_f), BF16), pltpu.VMEM((d_exp, dm), BF16)],
    )
    return pl.pallas_call(
        _experts_kernel,
        grid_spec=grid_spec,
        out_shape=jax.ShapeDtypeStruct((n_rows * SUBLANES, LANES), F32),
        compiler_params=_cparams(("arbitrary",)),
        name="experts",
    )(blk_exp, n_used, xs, w1, b1.reshape(n_exp, 1, two_f), w2, b2.reshape(n_exp, 1, dm))


def _combine_kernel(dest_ref, ys_hbm, h1_ref, gate_ref, g_ref, b_ref, o_ref, buf_ref, sem):
    tm = h1_ref.shape[0]

    def start(t, c):
        for k in range(TOP_K):
            pltpu.make_async_copy(_tile_rows(ys_hbm, dest_ref[k, t]), _tile_rows(buf_ref.at[k], t),
                                  sem).start(priority=k % 2)
        return c

    def wait(t, c):
        for k in range(TOP_K):
            pltpu.make_async_copy(_tile_rows(ys_hbm, 0), _tile_rows(buf_ref.at[0], 0), sem).wait()
        return c

    lax.fori_loop(0, tm, start, 0, unroll=DMA_UNROLL)
    lax.fori_loop(0, tm, wait, 0, unroll=DMA_UNROLL)
    gate = gate_ref[...]
    moe = jnp.zeros(h1_ref.shape, F32)
    for k in range(TOP_K):
        moe = moe + _from_tiles(buf_ref.at[k]) * gate[:, k:k + 1]
    o_ref[...] = _layer_norm(DN_ALPHA * h1_ref[...] + moe, g_ref[...], b_ref[...])


def _combine(ys, dest_t, h1, gate, ln_g, ln_b, tm=256):
    n, dm = h1.shape
    row = lambda i: (i, 0)
    fixed = lambda i: (0, 0)
    return pl.pallas_call(
        _combine_kernel,
        grid=(n // tm,),
        in_specs=[pl.BlockSpec((ROUTE_ROWS, tm), lambda i: (0, i), memory_space=pltpu.SMEM),
                  pl.BlockSpec(memory_space=pl.ANY),
                  pl.BlockSpec((tm, dm), row), pl.BlockSpec((tm, LANES), row),
                  pl.BlockSpec((1, dm), fixed), pl.BlockSpec((1, dm), fixed)],
        out_specs=pl.BlockSpec((tm, dm), row),
        out_shape=jax.ShapeDtypeStruct((n, dm), F32),
        scratch_shapes=[pltpu.VMEM((TOP_K, tm * SUBLANES, LANES), F32), pltpu.SemaphoreType.DMA(())],
        compiler_params=_cparams(("arbitrary",)),
        name="combine",
    )(dest_t, ys, h1, gate, ln_g.reshape(1, dm), ln_b.reshape(1, dm))


def _moe(h1, rl, w1, b1, w2, b2, ln_g, ln_b):
    n, dm = h1.shape
    eid_t, gate, rank_t, counts = _route(rl)
    counts = counts[0, :N_EXPERTS].astype(I32)
    padded = (counts + ROW_BLOCK - 1) // ROW_BLOCK * ROW_BLOCK
    pad_end = jnp.cumsum(padded)
    group_start = pad_end - padded
    n_blocks = n * TOP_K // ROW_BLOCK + N_EXPERTS
    n_used = (pad_end[-1] // ROW_BLOCK).astype(I32)
    blk = jnp.minimum(jnp.arange(n_blocks, dtype=I32), n_used - 1)
    blk_exp = jnp.sum((pad_end[None, :] <= (blk * ROW_BLOCK)[:, None]).astype(I32), axis=1)
    expert = jnp.arange(N_EXPERTS, dtype=I32)[:, None, None]
    dest_t = jnp.sum(jnp.where(eid_t[None] == expert, group_start[:, None, None], 0), axis=0) + rank_t
    xs = _dispatch(h1, dest_t, n_blocks * ROW_BLOCK)
    ys = _experts(xs, blk_exp, n_used.reshape(1), w1, b1, w2, b2)
    return _combine(ys, dest_t, h1, gate, ln_g, ln_b)


def kernel(x, mem, ln0_g, ln0_b, w_in, b_in, kv_norm_g, w_uk, w_uv, w_o_a, w_dw, b_dw, conv_ln_g, conv_ln_b, w_pw2, b_pw2, w_mem_k, w_mem_v, w_o_c, w_out, b_out, ln1_g, ln1_b, w_router, b_router, w1, b1, w2, b2, ln2_g, ln2_b):
    bsz, seq, dm = x.shape
    n = bsz * seq
    assert w_in.shape[0] == DEPTH == 1, "the stem LayerNorm is fused into the single layer's projection"
    l = 0
    gate_lo = w_in.shape[-1] - N_BRANCH * dm
    h0, qa, ckv, qidx, krep, widx, u, qc = _proj(x.reshape(n, dm), ln0_g, ln0_b, w_in[l, :, :gate_lo],
                                                   b_in[l, :gate_lo], kv_norm_g[l])
    oa = _dsa_attention(qidx, widx, qa, krep, ckv, w_uk[l], w_uv[l], bsz, seq).reshape(n, -1)
    u2 = _conv(u, w_dw[l], b_dw[l], conv_ln_g[l], conv_ln_b[l], bsz, seq).reshape(n, -1)
    oc = _mem_attention(qc, mem, w_mem_k[l], w_mem_v[l], bsz, seq).reshape(n, -1)
    h1, rl = _mix(h0, oa, u2, oc, w_in[l, :, gate_lo:], b_in[l, gate_lo:], w_o_a[l], w_pw2[l], b_pw2[l],
                  w_o_c[l], w_out[l], b_out[l], ln1_g[l], ln1_b[l], w_router[l], b_router[l])
    h2 = _moe(h1, rl, w1[l], b1[l], w2[l], b2[l], ln2_g[l], ln2_b[l])
    return h2.reshape(bsz, seq, dm)
```

```python
import jax
import jax.numpy as jnp
import numpy as np
from jax import lax
from jax.experimental import pallas as pl
from jax.experimental.pallas import tpu as pltpu

F32 = jnp.float32
BF16 = jnp.bfloat16
I32 = jnp.int32

CHUNK = 64
A_HEADS = 8
A_HEAD_DIM = 64
KV_RANK = 128
IDX_HEADS = 8
IDX_DIM = 32
TOPK_MAX = 256
CONV_WIDTH = 31
C_HEADS = 4
C_HEAD_DIM = 128
N_BRANCH = 3
N_EXPERTS = 32
TOP_K = 4
SWIGLU_LIMIT = 7.0
SWIGLU_ALPHA = 1.702
LN_EPS = 1e-5
NEG_INF = -1e30
DEPTH = 1
DN_ALPHA = (2 * DEPTH) ** 0.25
A_SCALE = A_HEAD_DIM ** -0.5
IDX_SCALE = (IDX_HEADS * IDX_DIM) ** -0.5
C_SCALE = C_HEAD_DIM ** -0.5

LANES = 128
SUBLANES = 8
Q_BLOCK = 128
KEY_TILE = 512
ATTEND_UNROLLS = (4, 2, 1)
COARSE_BITS = 20
FINE_KEEP = 4
FINE_CHAINS = 4
ROW_BLOCK = 512
VMEM_LIMIT = 56 * 1024 * 1024


def _cparams(sem, vmem=VMEM_LIMIT):
    return pltpu.CompilerParams(dimension_semantics=sem, vmem_limit_bytes=vmem)


def _layer_norm(x, g, b):
    xc = x - jnp.mean(x, -1, keepdims=True)
    var = jnp.mean(xc * xc, -1, keepdims=True)
    return xc * lax.rsqrt(var + LN_EPS) * g + b


def _dot(a, b):
    return jnp.dot(a, b, preferred_element_type=F32)


def _dot_nt(a, b):
    return lax.dot_general(a, b, (((1,), (1,)), ((), ())), preferred_element_type=F32)


_P_QA = (0, 512)
_P_CKV = (512, 640)
_P_QIDX = (640, 896)
_P_KREP = (896, 1152)
_P_WIDX = (1152, 1280)
_P_GLU_A = (1280, 1792)
_P_GLU_G = (1792, 2304)
_P_QC = (2304, 2816)
_P_COLS = 2816


def _proj_kernel(x_ref, g_ref, b_ref, w_ref, bias_ref, kvg_ref,
                 h0_ref, qa_ref, ckv_ref, qidx_ref, krep_ref, widx_ref, u_ref, qc_ref):
    h = _layer_norm(x_ref[...], g_ref[...], b_ref[...])
    h0_ref[...] = h
    hb = h.astype(BF16)

    def proj(span):
        lo, hi = span
        return _dot(hb, w_ref[:, lo:hi]) + bias_ref[:, lo:hi]

    qa_ref[...] = proj(_P_QA).astype(BF16)
    ckv = proj(_P_CKV)
    ckv = ckv * lax.rsqrt(jnp.mean(ckv * ckv, -1, keepdims=True) + LN_EPS) * kvg_ref[...]
    ckv_ref[...] = ckv.astype(BF16)
    qidx_ref[...] = proj(_P_QIDX).astype(BF16)
    krep_ref[...] = proj(_P_KREP).astype(BF16)
    widx_ref[...] = proj(_P_WIDX) * IDX_SCALE
    u_ref[...] = proj(_P_GLU_A) * jax.nn.sigmoid(proj(_P_GLU_G))
    qc_ref[...] = proj(_P_QC).astype(BF16)


def _proj(xt, ln_g, ln_b, w_in, b_in, kv_g, tm=512):
    n, dm = xt.shape
    o = 0
    parts = {}
    for name, width in (("qa", 512), ("ckv", 128), ("qidx", 256), ("kidx", 32), ("widx", 8),
                        ("glu", 1024), ("qc", 512)):
        parts[name] = (o, o + width)
        o += width

    def cols(a, name):
        lo, hi = parts[name]
        return a[..., lo:hi]

    def pack(a):
        pad = jnp.zeros(a.shape[:-1] + (LANES - 8,), a.dtype)
        glu = cols(a, "glu")
        return jnp.concatenate(
            [cols(a, "qa"), cols(a, "ckv"), cols(a, "qidx"),
             jnp.concatenate([cols(a, "kidx")] * IDX_HEADS, -1),
             cols(a, "widx"), pad, glu[..., :512], glu[..., 512:], cols(a, "qc")], -1)

    w = pack(w_in).astype(BF16)
    bias = pack(b_in).reshape(1, _P_COLS)
    row = lambda i: (i, 0)
    fixed = lambda i: (0, 0)
    outs = [(dm, F32), (512, BF16), (128, BF16), (256, BF16), (256, BF16), (128, F32), (512, F32), (512, BF16)]
    return pl.pallas_call(
        _proj_kernel,
        grid=(n // tm,),
        in_specs=[pl.BlockSpec((tm, dm), row), pl.BlockSpec((1, dm), fixed), pl.BlockSpec((1, dm), fixed),
                  pl.BlockSpec((dm, _P_COLS), fixed), pl.BlockSpec((1, _P_COLS), fixed),
                  pl.BlockSpec((1, KV_RANK), fixed)],
        out_specs=[pl.BlockSpec((tm, c), row) for c, _ in outs],
        out_shape=[jax.ShapeDtypeStruct((n, c), d) for c, d in outs],
        compiler_params=_cparams(("arbitrary",)),
        name="proj",
    )(xt, ln_g.reshape(1, dm), ln_b.reshape(1, dm), w, bias, kv_g.reshape(1, KV_RANK))


LOG2E = 1.4426950408889634


def _alibi_slope(h):
    return 2.0 ** (-(8.0 / A_HEADS) * (h + 1))


def _bf16_split(v):
    hi = float(np.asarray(v, dtype=BF16))
    return hi, float(np.asarray(v - hi, dtype=BF16))


def _attn_kernel(qidx_ref, widx_ref, qa_ref, krep_ref, ckv_ref, ext_ref, wuk_ref, wuv_ref, o_ref,
                 kp_ref, sc_ref, p_ref, acc_ref, m_ref):
    i = pl.program_id(1)
    t0 = i * Q_BLOCK

    @pl.when(i == 0)
    def _():
        kp_ref[:, :KV_RANK] = ckv_ref[0]
        kp_ref[:, KV_RANK:] = ext_ref[...]

    n_full = i // (KEY_TILE // Q_BLOCK)
    n_tiles = n_full + 1

    def key_rows(ref, j):
        return ref[pl.ds(pl.multiple_of(j * KEY_TILE, KEY_TILE), KEY_TILE), :]

    qi = qidx_ref[0]
    head_of_lane = lax.broadcasted_iota(I32, (1, IDX_HEADS * IDX_DIM), 1) // IDX_DIM
    qbd = jnp.concatenate([jnp.where(head_of_lane == h, qi, jnp.zeros_like(qi)) for h in range(IDX_HEADS)], 0)
    w_t = jnp.transpose(widx_ref[0])

    def score_tile(j):
        s = _dot_nt(key_rows(krep_ref.at[0], j), qbd)
        acc = jnp.zeros((KEY_TILE, Q_BLOCK), F32)
        for h in range(IDX_HEADS):
            acc = acc + jnp.maximum(s[:, h * Q_BLOCK:(h + 1) * Q_BLOCK], 0.0) * w_t[h:h + 1, :]
        return acc + 0.0

    def score_pair(jj, c):
        sc_ref[2 * jj] = score_tile(2 * jj)
        sc_ref[2 * jj + 1] = score_tile(2 * jj + 1)
        return c

    def score_single(j, c):
        sc_ref[j] = score_tile(j)
        return c

    lax.fori_loop(0, n_full // 2, score_pair, 0)
    lax.fori_loop(n_full // 2 * 2, n_full, score_single, 0)
    key_row = lax.broadcasted_iota(I32, (KEY_TILE, Q_BLOCK), 0)
    q_pos = t0 + lax.broadcasted_iota(I32, (KEY_TILE, Q_BLOCK), 1)
    key_last = n_full * KEY_TILE + key_row
    adm_last = (key_last // CHUNK) <= (q_pos // CHUNK)
    sc_ref[n_full] = jnp.where(adm_last, score_tile(n_full), -jnp.inf)

    def by_sublane(a):
        return a.reshape(KEY_TILE // SUBLANES, SUBLANES, Q_BLOCK)

    def scan_tiles(step, carry):
        carry = lax.fori_loop(0, n_tiles // 2, lambda jj, c: step(2 * jj + 1, step(2 * jj, c)), carry)
        return lax.fori_loop(n_tiles // 2 * 2, n_tiles, step, carry)

    def count_where(*preds):
        def step(j, cnts):
            s = sc_ref[j]
            return tuple(cnt + jnp.sum(by_sublane(jnp.where(pred(s, j * KEY_TILE), 1, 0)), axis=0)
                         for cnt, pred in zip(cnts, preds))
        cnts = scan_tiles(step, (jnp.zeros((SUBLANES, Q_BLOCK), I32),) * len(preds))
        cnts = [jnp.sum(c, axis=0, keepdims=True) for c in cnts]
        return cnts[0] if len(preds) == 1 else cnts

    def key_to_float(key):
        bits = key ^ ((key >> 31) & jnp.int32(0x7FFFFFFF))
        return lax.bitcast_convert_type(bits, F32)

    int_min = jnp.int32(-2 ** 31)

    def bit_step(b, prefix):
        cand = prefix + lax.shift_left(jnp.int32(1), jnp.int32(31) - b)
        cand_f = key_to_float(cand)
        return jnp.where(count_where(lambda s, base: s >= cand_f) >= TOPK_MAX, cand, prefix)

    coarse = lax.fori_loop(0, COARSE_BITS, bit_step, jnp.full((1, Q_BLOCK), int_min, I32))
    enough = coarse != int_min

    top_f = key_to_float(coarse + jnp.int32(1 << (32 - COARSE_BITS)))

    def insert(best, v):
        out = []
        for b in best:
            out.append(jnp.maximum(b, v))
            v = jnp.minimum(b, v)
        return out

    def merge(best, other):
        for v in other:
            best = insert(best, v)
        return best

    def fine_step(j, carry):
        cnt, lists = carry[0], [list(carry[1 + FINE_KEEP * c:1 + FINE_KEEP * (c + 1)]) for c in range(FINE_CHAINS)]
        s = sc_ref[j]
        above = s >= top_f
        cnt = cnt + jnp.sum(by_sublane(jnp.where(above, 1, 0)), axis=0)
        below = by_sublane(jnp.where(above, -jnp.inf, s))
        for g in range(KEY_TILE // SUBLANES):
            lists[g % FINE_CHAINS] = insert(lists[g % FINE_CHAINS], below[g])
        return (cnt,) + tuple(v for lst in lists for v in lst)

    lowest = jnp.full((SUBLANES, Q_BLOCK), -jnp.inf, F32)
    carry = scan_tiles(fine_step, (jnp.zeros((SUBLANES, Q_BLOCK), I32),) + (lowest,) * (FINE_KEEP * FINE_CHAINS))
    n_above = jnp.sum(carry[0], axis=0, keepdims=True)
    best = list(carry[1:1 + FINE_KEEP])
    for c in range(1, FINE_CHAINS):
        best = merge(best, carry[1 + FINE_KEEP * c:1 + FINE_KEEP * (c + 1)])
    top = [b[0:1] for b in best]
    for sub in range(1, SUBLANES):
        top = merge(top, [b[sub:sub + 1] for b in best])
    r = TOPK_MAX - n_above
    thr_fast = top[FINE_KEEP - 1]
    for k in range(FINE_KEEP - 1, 0, -1):
        thr_fast = jnp.where(r == k, top[k - 1], thr_fast)
    found = (r <= FINE_KEEP) | jnp.logical_not(enough)
    resolved = jnp.min(jnp.where(found, 1, 0)) > 0

    def remaining_bits(_):
        return key_to_float(lax.fori_loop(COARSE_BITS, 32, bit_step, coarse))

    def many_in_range(_):
        bottom = key_to_float(coarse)
        at_bottom = found | (count_where(lambda s, base: s > bottom) < TOPK_MAX)
        value = jnp.where(found, thr_fast, bottom)
        return lax.cond(jnp.min(jnp.where(at_bottom, 1, 0)) > 0, lambda _: value, remaining_bits, 0)

    thr = lax.cond(resolved, lambda _: thr_fast, many_in_range, 0)
    thr = jnp.where(enough, thr, -jnp.inf)

    n_gt, n_eq = count_where(lambda s, base: s > thr, lambda s, base: s == thr)
    need = TOPK_MAX - n_gt
    seq_bits = 13

    def tie_search(_):
        def step(b, lo):
            cand = lo + lax.shift_left(jnp.int32(1), jnp.int32(seq_bits) - b)
            below = count_where(lambda s, base: (s == thr) & ((base + key_row) < cand))
            return jnp.where(below < need, cand, lo)
        return lax.fori_loop(0, seq_bits + 1, step, jnp.zeros((1, Q_BLOCK), I32))

    overflow = jnp.max(jnp.where(n_gt + n_eq > TOPK_MAX, 1, 0)) > 0
    last_tie = lax.cond(overflow, tie_search, lambda _: jnp.full((1, Q_BLOCK), 2 ** 30, I32), 0)

    def mask_bias(j):
        s = sc_ref[j]
        tie_ok = jnp.where((j * KEY_TILE + key_row) <= last_tie, 0.0, NEG_INF)
        mb = jnp.where(s > thr, 0.0, jnp.where(s == thr, tie_ok, NEG_INF))
        return jnp.transpose(mb)

    qlat = _dot(qa_ref[0], wuk_ref[...]) * (A_SCALE * LOG2E)
    lane_e = lax.broadcasted_iota(I32, (Q_BLOCK, LANES), 1)
    t0f = t0.astype(F32)
    qp = []
    for h in range(A_HEADS):
        sl = _alibi_slope(h) * LOG2E
        (c_hi, c_lo), (o_hi, o_lo) = _bf16_split(CHUNK * sl), _bf16_split(sl)
        cols = (c_hi, o_hi, -sl * t0f, c_lo, o_lo)
        qext = jnp.zeros((Q_BLOCK, LANES), F32)
        for n, v in enumerate(cols):
            qext = jnp.where(lane_e == n, v, qext)
        qp.append(jnp.concatenate([qlat[:, h * KV_RANK:(h + 1) * KV_RANK], qext], 1))
    qp = jnp.concatenate(qp, 0).astype(BF16)

    m_ref[...] = jnp.full(m_ref.shape, NEG_INF, F32)
    acc_ref[...] = jnp.zeros(acc_ref.shape, F32)

    def attend(j, slot, last=False):
        kt = key_rows(kp_ref, j)
        logits = _dot_nt(qp, kt)
        mb = mask_bias(j)
        if last:
            row_t = t0 + lax.broadcasted_iota(I32, (Q_BLOCK, KEY_TILE), 0)
            key_s = n_full * KEY_TILE + lax.broadcasted_iota(I32, (Q_BLOCK, KEY_TILE), 1)
            mb = jnp.where((key_s // CHUNK) <= (row_t // CHUNK), mb, NEG_INF)
            ahead = jnp.maximum(key_s - row_t, 0).astype(F32)
        alphas = []
        for h in range(A_HEADS):
            l = logits[h * Q_BLOCK:(h + 1) * Q_BLOCK] + mb
            if last:
                l = l - (2.0 * LOG2E * _alibi_slope(h)) * ahead
            m_old = m_ref[h]
            m_new = jnp.maximum(m_old, jnp.broadcast_to(jnp.max(l, axis=1, keepdims=True), (Q_BLOCK, LANES)))
            m_ref[h] = m_new
            alphas.append(jnp.exp2(m_old - m_new))
            p = jnp.exp2(l - jnp.concatenate([m_new] * (KEY_TILE // LANES), 1))
            p_ref[slot, h * Q_BLOCK:(h + 1) * Q_BLOCK, :] = p.astype(BF16)
        pv = _dot(p_ref[slot], kt)
        for h in range(A_HEADS):
            acc_ref[h] = acc_ref[h] * jnp.concatenate([alphas[h]] * 2, 1) + pv[h * Q_BLOCK:(h + 1) * Q_BLOCK]

    def attend_group(width, first):
        def body(jj, c):
            for k in range(width):
                attend(first + width * jj + k, k % 2)
            return c
        return body

    done = 0
    for width in ATTEND_UNROLLS:
        trips = (n_full - done) // width
        lax.fori_loop(0, trips, attend_group(width, done), 0)
        done = done + trips * width
    attend(n_full, 0, last=True)

    outs = []
    for h in range(A_HEADS):
        a = acc_ref[h]
        outs.append(a[:, :KV_RANK] / a[:, KV_RANK + 2:KV_RANK + 3])
    olat = jnp.concatenate(outs, 1).astype(BF16)
    o_ref[0] = _dot(olat, wuv_ref[...]).astype(BF16)


def _dsa_attention(qidx, widx, qa, krep, ckv, w_uk, w_uv, bsz, seq):
    nq = seq // Q_BLOCK
    n_key_tiles = seq // KEY_TILE
    pos = lax.broadcasted_iota(I32, (seq, LANES), 0)
    lane = lax.broadcasted_iota(I32, (seq, LANES), 1)
    ext = jnp.where((lane == 0) | (lane == 3), pos // CHUNK,
                    jnp.where((lane == 1) | (lane == 4), pos % CHUNK, jnp.where(lane == 2, 1, 0)))
    ext = ext.astype(BF16)
    eye = jnp.eye(A_HEADS, dtype=F32)
    wuk_bd = jnp.einsum("hg,hrd->hdgr", eye, w_uk).reshape(A_HEADS * A_HEAD_DIM, A_HEADS * KV_RANK)
    wuv_bd = jnp.einsum("hg,hrd->hrgd", eye, w_uv).reshape(A_HEADS * KV_RANK, A_HEADS * A_HEAD_DIM)
    blk = lambda b, i: (b, i, 0)
    whole = lambda b, i: (b, 0, 0)
    fixed = lambda b, i: (0, 0)
    r3 = lambda a: a.reshape(bsz, seq, a.shape[-1])
    return pl.pallas_call(
        _attn_kernel,
        grid=(bsz, nq),
        in_specs=[pl.BlockSpec((1, Q_BLOCK, 256), blk), pl.BlockSpec((1, Q_BLOCK, LANES), blk),
                  pl.BlockSpec((1, Q_BLOCK, 512), blk),
                  pl.BlockSpec((1, seq, 256), whole), pl.BlockSpec((1, seq, KV_RANK), whole),
                  pl.BlockSpec((seq, LANES), fixed),
                  pl.BlockSpec((512, 1024), fixed), pl.BlockSpec((1024, 512), fixed)],
        out_specs=pl.BlockSpec((1, Q_BLOCK, 512), blk),
        out_shape=jax.ShapeDtypeStruct((bsz, seq, 512), BF16),
        scratch_shapes=[pltpu.VMEM((seq, 2 * KV_RANK), BF16),
                        pltpu.VMEM((n_key_tiles, KEY_TILE, Q_BLOCK), F32),
                        pltpu.VMEM((2, A_HEADS * Q_BLOCK, KEY_TILE), BF16),
                        pltpu.VMEM((A_HEADS, Q_BLOCK, 2 * KV_RANK), F32),
                        pltpu.VMEM((A_HEADS, Q_BLOCK, LANES), F32)],
        compiler_params=_cparams(("arbitrary", "arbitrary")),
        name="attn",
    )(r3(qidx), r3(widx), r3(qa), r3(krep), r3(ckv), ext, wuk_bd.astype(BF16), wuv_bd.astype(BF16))


CONV_HALO = 32
CONV_PHASES = SUBLANES


def _conv_kernel(cur_ref, prev_ref, w_ref, b_ref, g_ref, beta_ref, o_ref, win_ref, shift_ref):
    j = pl.program_id(1)
    tq = cur_ref.shape[1]
    halo = prev_ref[0, tq - CONV_HALO:, :]
    win_ref[:CONV_HALO] = jnp.where(j > 0, halo, jnp.zeros_like(halo))
    win_ref[CONV_HALO:] = cur_ref[0]
    acc = jnp.zeros((tq, cur_ref.shape[2]), F32)
    first = CONV_HALO - (CONV_WIDTH - 1)
    for r in range(CONV_PHASES):
        taps = range(r, CONV_WIDTH, CONV_PHASES)
        rows = taps[-1] - r + tq
        shift_ref[r, :rows] = win_ref[first + r:first + r + rows, :]
        for k in taps:
            acc = acc + shift_ref[r, k - r:k - r + tq] * w_ref[k:k + 1, :]
    y = _layer_norm(acc + b_ref[...], g_ref[...], beta_ref[...])
    o_ref[0] = (y * jax.nn.sigmoid(y)).astype(BF16)


def _conv(u, w_dw, b_dw, ln_g, ln_b, bsz, seq, tq=512):
    c = u.shape[-1]
    u3 = u.reshape(bsz, seq, c)
    fixed = lambda b, j: (0, 0)
    return pl.pallas_call(
        _conv_kernel,
        grid=(bsz, seq // tq),
        in_specs=[pl.BlockSpec((1, tq, c), lambda b, j: (b, j, 0)),
                  pl.BlockSpec((1, tq, c), lambda b, j: (b, jnp.maximum(j - 1, 0), 0)),
                  pl.BlockSpec((CONV_WIDTH, c), fixed), pl.BlockSpec((1, c), fixed),
                  pl.BlockSpec((1, c), fixed), pl.BlockSpec((1, c), fixed)],
        out_specs=pl.BlockSpec((1, tq, c), lambda b, j: (b, j, 0)),
        out_shape=jax.ShapeDtypeStruct((bsz, seq, c), BF16),
        scratch_shapes=[pltpu.VMEM((tq + CONV_HALO, c), F32),
                        pltpu.VMEM((CONV_PHASES, tq + CONV_HALO, c), F32)],
        compiler_params=_cparams(("arbitrary", "arbitrary")),
        name="conv",
    )(u3, u3, w_dw, b_dw.reshape(1, c), ln_g.reshape(1, c), ln_b.reshape(1, c))


def _memattn_kernel(q_ref, mem_ref, wk_ref, wv_ref, o_ref, k_ref, v_ref):
    @pl.when(pl.program_id(1) == 0)
    def _():
        mb = mem_ref[0].astype(BF16)
        k_ref[...] = _dot(mb, wk_ref[...]).astype(BF16)
        v_ref[...] = _dot(mb, wv_ref[...]).astype(BF16)

    q = q_ref[0]
    outs = []
    for h in range(C_HEADS):
        sl = slice(h * C_HEAD_DIM, (h + 1) * C_HEAD_DIM)
        s = _dot_nt(q[:, sl], k_ref[:, sl]) * C_SCALE
        e = jnp.exp(s - jnp.max(s, axis=1, keepdims=True))
        p = e / jnp.sum(e, axis=1, keepdims=True)
        outs.append(_dot(p.astype(BF16), v_ref[:, sl]))
    o_ref[0] = jnp.concatenate(outs, 1).astype(BF16)


def _mem_attention(qc, mem, w_mem_k, w_mem_v, bsz, seq, tq=512):
    n_mem, dm = mem.shape[1:]
    cw = C_HEADS * C_HEAD_DIM
    fixed = lambda b, j: (0, 0)
    return pl.pallas_call(
        _memattn_kernel,
        grid=(bsz, seq // tq),
        in_specs=[pl.BlockSpec((1, tq, cw), lambda b, j: (b, j, 0)),
                  pl.BlockSpec((1, n_mem, dm), lambda b, j: (b, 0, 0)),
                  pl.BlockSpec((dm, cw), fixed), pl.BlockSpec((dm, cw), fixed)],
        out_specs=pl.BlockSpec((1, tq, cw), lambda b, j: (b, j, 0)),
        out_shape=jax.ShapeDtypeStruct((bsz, seq, cw), BF16),
        scratch_shapes=[pltpu.VMEM((n_mem, cw), BF16), pltpu.VMEM((n_mem, cw), BF16)],
        compiler_params=_cparams(("arbitrary", "arbitrary")),
        name="memattn",
    )(qc.reshape(bsz, seq, cw), mem, w_mem_k.astype(BF16), w_mem_v.astype(BF16))


def _mix_kernel(h0_ref, oa_ref, u2_ref, oc_ref, wg_ref, bg_ref, woa_ref, wpw_ref, bpw_ref, woc_ref,
                wout_ref, bout_ref, g_ref, b_ref, wr_ref, br_ref, h1_ref, rl_ref):
    h0 = h0_ref[...]
    hb = h0.astype(BF16)
    dm = h0.shape[1]
    y = (_dot(oa_ref[...], woa_ref[...]),
         _dot(u2_ref[...], wpw_ref[...]) + bpw_ref[...],
         _dot(oc_ref[...], woc_ref[...]))
    mixed = jnp.zeros_like(h0)
    for k in range(N_BRANCH):
        gate = jax.nn.sigmoid(_dot(hb, wg_ref[:, k * dm:(k + 1) * dm]) + bg_ref[:, k * dm:(k + 1) * dm])
        mixed = mixed + gate * y[k]
    out = _dot(mixed.astype(BF16), wout_ref[...]) + bout_ref[...]
    h1 = _layer_norm(DN_ALPHA * h0 + out, g_ref[...], b_ref[...])
    h1_ref[...] = h1
    rl_ref[...] = _dot(h1.astype(BF16), wr_ref[...]) + br_ref[...]


def _mix(h0, oa, u2, oc, w_gate, b_gate, w_o_a, w_pw2, b_pw2, w_o_c, w_out, b_out, ln_g, ln_b, w_router, b_router,
         tm=256):
    n, dm = h0.shape
    row = lambda i: (i, 0)
    fixed = lambda i: (0, 0)
    wr = jnp.zeros((dm, LANES), F32).at[:, :N_EXPERTS].set(w_router).astype(BF16)
    br = jnp.zeros((1, LANES), F32).at[0, :N_EXPERTS].set(b_router)
    full = lambda a: pl.BlockSpec(a.shape, fixed)
    ws = [w_gate.astype(BF16), b_gate.reshape(1, -1), w_o_a.astype(BF16), w_pw2.astype(BF16), b_pw2.reshape(1, -1),
          w_o_c.astype(BF16), w_out.astype(BF16), b_out.reshape(1, -1), ln_g.reshape(1, -1), ln_b.reshape(1, -1),
          wr, br]
    return pl.pallas_call(
        _mix_kernel,
        grid=(n // tm,),
        in_specs=[pl.BlockSpec((tm, dm), row), pl.BlockSpec((tm, 512), row), pl.BlockSpec((tm, 512), row),
                  pl.BlockSpec((tm, 512), row)] + [full(a) for a in ws],
        out_specs=[pl.BlockSpec((tm, dm), row), pl.BlockSpec((tm, LANES), row)],
        out_shape=[jax.ShapeDtypeStruct((n, dm), F32), jax.ShapeDtypeStruct((n, LANES), F32)],
        compiler_params=_cparams(("arbitrary",)),
        name="mix",
    )(h0, oa, u2, oc, *ws)


def _route_kernel(rl_ref, eid_ref, gate_ref, rank_ref, cnt_ref, carry_ref):
    @pl.when(pl.program_id(0) == 0)
    def _():
        carry_ref[...] = jnp.zeros_like(carry_ref)

    tm = rl_ref.shape[0]
    lane = lax.broadcasted_iota(I32, (tm, LANES), 1)
    vals = jnp.where(lane < N_EXPERTS, rl_ref[...], -jnp.inf)
    top_v, top_i, sels = [], [], []
    for _ in range(TOP_K):
        m = jnp.max(vals, axis=1, keepdims=True)
        idx = jnp.min(jnp.where(vals == m, lane, LANES), axis=1, keepdims=True)
        sel = lane == idx
        top_v.append(m)
        top_i.append(idx)
        sels.append(sel)
        vals = jnp.where(sel, -jnp.inf, vals)
    es = [jnp.exp(v - top_v[0]) for v in top_v]
    denom = es[0] + es[1] + es[2] + es[3]
    onehot = jnp.zeros((tm, LANES), F32)
    for sel in sels:
        onehot = onehot + jnp.where(sel, 1.0, 0.0)
    r = lax.broadcasted_iota(I32, (tm, tm), 0)
    c = lax.broadcasted_iota(I32, (tm, tm), 1)
    tri = jnp.where(c < r, 1.0, 0.0).astype(BF16)
    before = _dot(tri, onehot.astype(BF16)) + carry_ref[...]
    carry = carry_ref[...] + jnp.sum(onehot, axis=0, keepdims=True)
    carry_ref[...] = carry
    cnt_ref[...] = carry
    eid = jnp.zeros((tm, LANES), I32)
    gate = jnp.zeros((tm, LANES), F32)
    rank = jnp.zeros((tm, LANES), F32)
    for k in range(TOP_K):
        eid = jnp.where(lane == k, top_i[k], eid)
        gate = jnp.where(lane == k, es[k] / denom, gate)
        rank = jnp.where(lane == k, jnp.sum(jnp.where(sels[k], before, 0.0), axis=1, keepdims=True), rank)
    eid_ref[...] = jnp.transpose(eid)[:ROUTE_ROWS]
    gate_ref[...] = gate
    rank_ref[...] = jnp.transpose(rank.astype(I32))[:ROUTE_ROWS]


ROUTE_ROWS = SUBLANES


def _route(rl, tm=512):
    n = rl.shape[0]
    row = lambda i: (i, 0)
    col = lambda i: (0, i)
    return pl.pallas_call(
        _route_kernel,
        grid=(n // tm,),
        in_specs=[pl.BlockSpec((tm, LANES), row)],
        out_specs=[pl.BlockSpec((ROUTE_ROWS, tm), col), pl.BlockSpec((tm, LANES), row),
                   pl.BlockSpec((ROUTE_ROWS, tm), col), pl.BlockSpec((1, LANES), lambda i: (0, 0))],
        out_shape=[jax.ShapeDtypeStruct((ROUTE_ROWS, n), I32), jax.ShapeDtypeStruct((n, LANES), F32),
                   jax.ShapeDtypeStruct((ROUTE_ROWS, n), I32), jax.ShapeDtypeStruct((1, LANES), F32)],
        scratch_shapes=[pltpu.VMEM((1, LANES), F32)],
        compiler_params=_cparams(("arbitrary",)),
        name="route",
    )(rl)


DMA_UNROLL = 4


def _to_tiles(x, tile_ref):
    for s in range(SUBLANES):
        tile_ref[pl.ds(s, x.shape[0], stride=SUBLANES), :] = x[:, s * LANES:(s + 1) * LANES]


def _from_tiles(tile_ref):
    rows = tile_ref.shape[0] // SUBLANES
    return jnp.concatenate([tile_ref[pl.ds(s, rows, stride=SUBLANES), :] for s in range(SUBLANES)], axis=1)


def _tile_rows(ref, row):
    return ref.at[pl.ds(pl.multiple_of(row * SUBLANES, SUBLANES), SUBLANES)]


def _dispatch_kernel(dest_ref, h1_ref, xs_in_hbm, xs_hbm, stage_ref, sem):
    del xs_in_hbm
    tm = h1_ref.shape[0]
    _to_tiles(h1_ref[...], stage_ref)

    def start(t, c):
        for k in range(TOP_K):
            pltpu.make_async_copy(_tile_rows(stage_ref, t), _tile_rows(xs_hbm, dest_ref[k, t]),
                                  sem).start(priority=k % 2)
        return c

    def wait(t, c):
        for k in range(TOP_K):
            pltpu.make_async_copy(_tile_rows(stage_ref, 0), _tile_rows(xs_hbm, 0), sem).wait()
        return c

    lax.fori_loop(0, tm, start, 0, unroll=DMA_UNROLL)
    lax.fori_loop(0, tm, wait, 0, unroll=DMA_UNROLL)


def _dispatch(h1, dest_t, n_rows, tm=256):
    n, dm = h1.shape
    xs0 = jnp.zeros((n_rows * SUBLANES, LANES), F32)
    return pl.pallas_call(
        _dispatch_kernel,
        grid=(n // tm,),
        in_specs=[pl.BlockSpec((ROUTE_ROWS, tm), lambda i: (0, i), memory_space=pltpu.SMEM),
                  pl.BlockSpec((tm, dm), lambda i: (i, 0)), pl.BlockSpec(memory_space=pl.ANY)],
        out_specs=pl.BlockSpec(memory_space=pl.ANY),
        out_shape=jax.ShapeDtypeStruct((n_rows * SUBLANES, LANES), F32),
        scratch_shapes=[pltpu.VMEM((tm * SUBLANES, LANES), F32), pltpu.SemaphoreType.DMA(())],
        input_output_aliases={2: 0},
        compiler_params=_cparams(("arbitrary",)),
        name="dispatch",
    )(dest_t, h1, xs0)


def _experts_kernel(blk_exp_ref, n_used_ref, xs_ref, w1_ref, b1_ref, w2_ref, b2_ref, ys_ref, w1b_ref, w2b_ref):
    b = pl.program_id(0)
    prev = blk_exp_ref[jnp.maximum(b - 1, 0)]
    used = b < n_used_ref[0]

    @pl.when(used & ((b == 0) | (blk_exp_ref[b] != prev)))
    def _():
        w1b_ref[...] = w1_ref[0].astype(BF16)
        w2b_ref[...] = w2_ref[0].astype(BF16)

    @pl.when(used)
    def _():
        d_exp = w2b_ref.shape[0]
        hdn = _dot(_from_tiles(xs_ref).astype(BF16), w1b_ref[...]) + b1_ref[0]
        g = jnp.minimum(hdn[:, :d_exp], SWIGLU_LIMIT)
        u = jnp.clip(hdn[:, d_exp:], -SWIGLU_LIMIT, SWIGLU_LIMIT)
        act = (u + 1.0) * (g * jax.nn.sigmoid(SWIGLU_ALPHA * g))
        _to_tiles(_dot(act.astype(BF16), w2b_ref[...]) + b2_ref[0], ys_ref)

    @pl.when(jnp.logical_not(used))
    def _():
        ys_ref[...] = jnp.zeros_like(ys_ref)


def _experts(xs, blk_exp, n_used, w1, b1, w2, b2):
    n_rows = xs.shape[0] // SUBLANES
    n_exp, dm, two_f = w1.shape
    d_exp = two_f // 2
    grid_spec = pltpu.PrefetchScalarGridSpec(
        num_scalar_prefetch=2,
        grid=(n_rows // ROW_BLOCK,),
        in_specs=[pl.BlockSpec((ROW_BLOCK * SUBLANES, LANES), lambda b, be, nu: (b, 0)),
                  pl.BlockSpec((1, dm, two_f), lambda b, be, nu: (be[b], 0, 0)),
                  pl.BlockSpec((1, 1, two_f), lambda b, be, nu: (be[b], 0, 0)),
                  pl.BlockSpec((1, d_exp, dm), lambda b, be, nu: (be[b], 0, 0)),
                  pl.BlockSpec((1, 1, dm), lambda b, be, nu: (be[b], 0, 0))],
        out_specs=pl.BlockSpec((ROW_BLOCK * SUBLANES, LANES), lambda b, be, nu: (b, 0)),
        scratch_shapes=[pltpu.VMEM((dm, two_f), BF16), pltpu.VMEM((d_exp, dm), BF16)],
    )
    return pl.pallas_call(
        _experts_kernel,
        grid_spec=grid_spec,
        out_shape=jax.ShapeDtypeStruct((n_rows * SUBLANES, LANES), F32),
        compiler_params=_cparams(("arbitrary",)),
        name="experts",
    )(blk_exp, n_used, xs, w1, b1.reshape(n_exp, 1, two_f), w2, b2.reshape(n_exp, 1, dm))


def _combine_kernel(dest_ref, ys_hbm, h1_ref, gate_ref, g_ref, b_ref, o_ref, buf_ref, sem):
    tm = h1_ref.shape[0]

    def start(t, c):
        for k in range(TOP_K):
            pltpu.make_async_copy(_tile_rows(ys_hbm, dest_ref[k, t]), _tile_rows(buf_ref.at[k], t),
                                  sem).start(priority=k % 2)
        return c

    def wait(t, c):
        for k in range(TOP_K):
            pltpu.make_async_copy(_tile_rows(ys_hbm, 0), _tile_rows(buf_ref.at[0], 0), sem).wait()
        return c

    lax.fori_loop(0, tm, start, 0, unroll=DMA_UNROLL)
    lax.fori_loop(0, tm, wait, 0, unroll=DMA_UNROLL)
    gate = gate_ref[...]
    moe = jnp.zeros(h1_ref.shape, F32)
    for k in range(TOP_K):
        moe = moe + _from_tiles(buf_ref.at[k]) * gate[:, k:k + 1]
    o_ref[...] = _layer_norm(DN_ALPHA * h1_ref[...] + moe, g_ref[...], b_ref[...])


def _combine(ys, dest_t, h1, gate, ln_g, ln_b, tm=256):
    n, dm = h1.shape
    row = lambda i: (i, 0)
    fixed = lambda i: (0, 0)
    return pl.pallas_call(
        _combine_kernel,
        grid=(n // tm,),
        in_specs=[pl.BlockSpec((ROUTE_ROWS, tm), lambda i: (0, i), memory_space=pltpu.SMEM),
                  pl.BlockSpec(memory_space=pl.ANY),
                  pl.BlockSpec((tm, dm), row), pl.BlockSpec((tm, LANES), row),
                  pl.BlockSpec((1, dm), fixed), pl.BlockSpec((1, dm), fixed)],
        out_specs=pl.BlockSpec((tm, dm), row),
        out_shape=jax.ShapeDtypeStruct((n, dm), F32),
        scratch_shapes=[pltpu.VMEM((TOP_K, tm * SUBLANES, LANES), F32), pltpu.SemaphoreType.DMA(())],
        compiler_params=_cparams(("arbitrary",)),
        name="combine",
    )(dest_t, ys, h1, gate, ln_g.reshape(1, dm), ln_b.reshape(1, dm))


def _moe(h1, rl, w1, b1, w2, b2, ln_g, ln_b):
    n, dm = h1.shape
    eid_t, gate, rank_t, counts = _route(rl)
    counts = counts[0, :N_EXPERTS].astype(I32)
    padded = (counts + ROW_BLOCK - 1) // ROW_BLOCK * ROW_BLOCK
    pad_end = jnp.cumsum(padded)
    group_start = pad_end - padded
    n_blocks = n * TOP_K // ROW_BLOCK + N_EXPERTS
    n_used = (pad_end[-1] // ROW_BLOCK).astype(I32)
    blk = jnp.minimum(jnp.arange(n_blocks, dtype=I32), n_used - 1)
    blk_exp = jnp.sum((pad_end[None, :] <= (blk * ROW_BLOCK)[:, None]).astype(I32), axis=1)
    expert = jnp.arange(N_EXPERTS, dtype=I32)[:, None, None]
    dest_t = jnp.sum(jnp.where(eid_t[None] == expert, group_start[:, None, None], 0), axis=0) + rank_t
    xs = _dispatch(h1, dest_t, n_blocks * ROW_BLOCK)
    ys = _experts(xs, blk_exp, n_used.reshape(1), w1, b1, w2, b2)
    return _combine(ys, dest_t, h1, gate, ln_g, ln_b)


def kernel(x, mem, ln0_g, ln0_b, w_in, b_in, kv_norm_g, w_uk, w_uv, w_o_a, w_dw, b_dw, conv_ln_g, conv_ln_b, w_pw2, b_pw2, w_mem_k, w_mem_v, w_o_c, w_out, b_out, ln1_g, ln1_b, w_router, b_router, w1, b1, w2, b2, ln2_g, ln2_b):
    bsz, seq, dm = x.shape
    n = bsz * seq
    assert w_in.shape[0] == DEPTH == 1, "the stem LayerNorm is fused into the single layer's projection"
    l = 0
    gate_lo = w_in.shape[-1] - N_BRANCH * dm
    h0, qa, ckv, qidx, krep, widx, u, qc = _proj(x.reshape(n, dm), ln0_g, ln0_b, w_in[l, :, :gate_lo],
                                                   b_in[l, :gate_lo], kv_norm_g[l])
    oa = _dsa_attention(qidx, widx, qa, krep, ckv, w_uk[l], w_uv[l], bsz, seq).reshape(n, -1)
    u2 = _conv(u, w_dw[l], b_dw[l], conv_ln_g[l], conv_ln_b[l], bsz, seq).reshape(n, -1)
    oc = _mem_attention(qc, mem, w_mem_k[l], w_mem_v[l], bsz, seq).reshape(n, -1)
    h1, rl = _mix(h0, oa, u2, oc, w_in[l, :, gate_lo:], b_in[l, gate_lo:], w_o_a[l], w_pw2[l], b_pw2[l],
                  w_o_c[l], w_out[l], b_out[l], ln1_g[l], ln1_b[l], w_router[l], b_router[l])
    h2 = _moe(h1, rl, w1[l], b1[l], w2[l], b2[l], ln2_g[l], ln2_b[l])
    return h2.reshape(bsz, seq, dm)
```

```python
import jax
import jax.numpy as jnp
import numpy as np
from jax import lax
from jax.experimental import pallas as pl
from jax.experimental.pallas import tpu as pltpu

F32 = jnp.float32
BF16 = jnp.bfloat16
I32 = jnp.int32

CHUNK = 64
A_HEADS = 8
A_HEAD_DIM = 64
KV_RANK = 128
IDX_HEADS = 8
IDX_DIM = 32
TOPK_MAX = 256
CONV_WIDTH = 31
C_HEADS = 4
C_HEAD_DIM = 128
N_BRANCH = 3
N_EXPERTS = 32
TOP_K = 4
SWIGLU_LIMIT = 7.0
SWIGLU_ALPHA = 1.702
LN_EPS = 1e-5
NEG_INF = -1e30
DEPTH = 1
DN_ALPHA = (2 * DEPTH) ** 0.25
A_SCALE = A_HEAD_DIM ** -0.5
IDX_SCALE = (IDX_HEADS * IDX_DIM) ** -0.5
C_SCALE = C_HEAD_DIM ** -0.5

LANES = 128
SUBLANES = 8
Q_BLOCK = 128
KEY_TILE = 512
ATTEND_UNROLLS = (4, 2, 1)
COARSE_BITS = 20
FINE_KEEP = 4
FINE_CHAINS = 4
ROW_BLOCK = 512
VMEM_LIMIT = 56 * 1024 * 1024


def _cparams(sem, vmem=VMEM_LIMIT):
    return pltpu.CompilerParams(dimension_semantics=sem, vmem_limit_bytes=vmem)


def _layer_norm(x, g, b):
    xc = x - jnp.mean(x, -1, keepdims=True)
    var = jnp.mean(xc * xc, -1, keepdims=True)
    return xc * lax.rsqrt(var + LN_EPS) * g + b


def _dot(a, b):
    return jnp.dot(a, b, preferred_element_type=F32)


def _dot_nt(a, b):
    return lax.dot_general(a, b, (((1,), (1,)), ((), ())), preferred_element_type=F32)


_P_QA = (0, 512)
_P_CKV = (512, 640)
_P_QIDX = (640, 896)
_P_KREP = (896, 1152)
_P_WIDX = (1152, 1280)
_P_GLU_A = (1280, 1792)
_P_GLU_G = (1792, 2304)
_P_QC = (2304, 2816)
_P_COLS = 2816


def _proj_kernel(x_ref, g_ref, b_ref, w_ref, bias_ref, kvg_ref,
                 h0_ref, qa_ref, ckv_ref, qidx_ref, krep_ref, widx_ref, u_ref, qc_ref):
    h = _layer_norm(x_ref[...], g_ref[...], b_ref[...])
    h0_ref[...] = h
    hb = h.astype(BF16)

    def proj(span):
        lo, hi = span
        return _dot(hb, w_ref[:, lo:hi]) + bias_ref[:, lo:hi]

    qa_ref[...] = proj(_P_QA).astype(BF16)
    ckv = proj(_P_CKV)
    ckv = ckv * lax.rsqrt(jnp.mean(ckv * ckv, -1, keepdims=True) + LN_EPS) * kvg_ref[...]
    ckv_ref[...] = ckv.astype(BF16)
    qidx_ref[...] = proj(_P_QIDX).astype(BF16)
    krep_ref[...] = proj(_P_KREP).astype(BF16)
    widx_ref[...] = proj(_P_WIDX) * IDX_SCALE
    u_ref[...] = proj(_P_GLU_A) * jax.nn.sigmoid(proj(_P_GLU_G))
    qc_ref[...] = proj(_P_QC).astype(BF16)


def _proj(xt, ln_g, ln_b, w_in, b_in, kv_g, tm=512):
    n, dm = xt.shape
    o = 0
    parts = {}
    for name, width in (("qa", 512), ("ckv", 128), ("qidx", 256), ("kidx", 32), ("widx", 8),
                        ("glu", 1024), ("qc", 512)):
        parts[name] = (o, o + width)
        o += width

    def cols(a, name):
        lo, hi = parts[name]
        return a[..., lo:hi]

    def pack(a):
        pad = jnp.zeros(a.shape[:-1] + (LANES - 8,), a.dtype)
        glu = cols(a, "glu")
        return jnp.concatenate(
            [cols(a, "qa"), cols(a, "ckv"), cols(a, "qidx"),
             jnp.concatenate([cols(a, "kidx")] * IDX_HEADS, -1),
             cols(a, "widx"), pad, glu[..., :512], glu[..., 512:], cols(a, "qc")], -1)

    w = pack(w_in).astype(BF16)
    bias = pack(b_in).reshape(1, _P_COLS)
    row = lambda i: (i, 0)
    fixed = lambda i: (0, 0)
    outs = [(dm, F32), (512, BF16), (128, BF16), (256, BF16), (256, BF16), (128, F32), (512, F32), (512, BF16)]
    return pl.pallas_call(
        _proj_kernel,
        grid=(n // tm,),
        in_specs=[pl.BlockSpec((tm, dm), row), pl.BlockSpec((1, dm), fixed), pl.BlockSpec((1, dm), fixed),
                  pl.BlockSpec((dm, _P_COLS), fixed), pl.BlockSpec((1, _P_COLS), fixed),
                  pl.BlockSpec((1, KV_RANK), fixed)],
        out_specs=[pl.BlockSpec((tm, c), row) for c, _ in outs],
        out_shape=[jax.ShapeDtypeStruct((n, c), d) for c, d in outs],
        compiler_params=_cparams(("arbitrary",)),
        name="proj",
    )(xt, ln_g.reshape(1, dm), ln_b.reshape(1, dm), w, bias, kv_g.reshape(1, KV_RANK))


LOG2E = 1.4426950408889634


def _alibi_slope(h):
    return 2.0 ** (-(8.0 / A_HEADS) * (h + 1))


def _bf16_split(v):
    hi = float(np.asarray(v, dtype=BF16))
    return hi, float(np.asarray(v - hi, dtype=BF16))


def _attn_kernel(qidx_ref, widx_ref, qa_ref, krep_ref, ckv_ref, ext_ref, wuk_ref, wuv_ref, o_ref,
                 kp_ref, sc_ref, p_ref, acc_ref, m_ref):
    i = pl.program_id(1)
    t0 = i * Q_BLOCK

    @pl.when(i == 0)
    def _():
        kp_ref[:, :KV_RANK] = ckv_ref[0]
        kp_ref[:, KV_RANK:] = ext_ref[...]

    n_full = i // (KEY_TILE // Q_BLOCK)
    n_tiles = n_full + 1

    def key_rows(ref, j):
        return ref[pl.ds(pl.multiple_of(j * KEY_TILE, KEY_TILE), KEY_TILE), :]

    qi = qidx_ref[0]
    head_of_lane = lax.broadcasted_iota(I32, (1, IDX_HEADS * IDX_DIM), 1) // IDX_DIM
    qbd = jnp.concatenate([jnp.where(head_of_lane == h, qi, jnp.zeros_like(qi)) for h in range(IDX_HEADS)], 0)
    w_t = jnp.transpose(widx_ref[0])

    def score_tile(j):
        s = _dot_nt(key_rows(krep_ref.at[0], j), qbd)
        acc = jnp.zeros((KEY_TILE, Q_BLOCK), F32)
        for h in range(IDX_HEADS):
            acc = acc + jnp.maximum(s[:, h * Q_BLOCK:(h + 1) * Q_BLOCK], 0.0) * w_t[h:h + 1, :]
        return acc + 0.0

    def score_pair(jj, c):
        sc_ref[2 * jj] = score_tile(2 * jj)
        sc_ref[2 * jj + 1] = score_tile(2 * jj + 1)
        return c

    def score_single(j, c):
        sc_ref[j] = score_tile(j)
        return c

    lax.fori_loop(0, n_full // 2, score_pair, 0)
    lax.fori_loop(n_full // 2 * 2, n_full, score_single, 0)
    key_row = lax.broadcasted_iota(I32, (KEY_TILE, Q_BLOCK), 0)
    q_pos = t0 + lax.broadcasted_iota(I32, (KEY_TILE, Q_BLOCK), 1)
    key_last = n_full * KEY_TILE + key_row
    adm_last = (key_last // CHUNK) <= (q_pos // CHUNK)
    sc_ref[n_full] = jnp.where(adm_last, score_tile(n_full), -jnp.inf)

    def by_sublane(a):
        return a.reshape(KEY_TILE // SUBLANES, SUBLANES, Q_BLOCK)

    def scan_tiles(step, carry):
        carry = lax.fori_loop(0, n_tiles // 2, lambda jj, c: step(2 * jj + 1, step(2 * jj, c)), carry)
        return lax.fori_loop(n_tiles // 2 * 2, n_tiles, step, carry)

    def count_where(*preds):
        def step(j, cnts):
            s = sc_ref[j]
            return tuple(cnt + jnp.sum(by_sublane(jnp.where(pred(s, j * KEY_TILE), 1, 0)), axis=0)
                         for cnt, pred in zip(cnts, preds))
        cnts = scan_tiles(step, (jnp.zeros((SUBLANES, Q_BLOCK), I32),) * len(preds))
        cnts = [jnp.sum(c, axis=0, keepdims=True) for c in cnts]
        return cnts[0] if len(preds) == 1 else cnts

    def key_to_float(key):
        bits = key ^ ((key >> 31) & jnp.int32(0x7FFFFFFF))
        return lax.bitcast_convert_type(bits, F32)

    int_min = jnp.int32(-2 ** 31)

    def bit_step(b, prefix):
        cand = prefix + lax.shift_left(jnp.int32(1), jnp.int32(31) - b)
        cand_f = key_to_float(cand)
        return jnp.where(count_where(lambda s, base: s >= cand_f) >= TOPK_MAX, cand, prefix)

    coarse = lax.fori_loop(0, COARSE_BITS, bit_step, jnp.full((1, Q_BLOCK), int_min, I32))
    enough = coarse != int_min

    top_f = key_to_float(coarse + jnp.int32(1 << (32 - COARSE_BITS)))

    def insert(best, v):
        out = []
        for b in best:
            out.append(jnp.maximum(b, v))
            v = jnp.minimum(b, v)
        return out

    def merge(best, other):
        for v in other:
            best = insert(best, v)
        return best

    def fine_step(j, carry):
        cnt, lists = carry[0], [list(carry[1 + FINE_KEEP * c:1 + FINE_KEEP * (c + 1)]) for c in range(FINE_CHAINS)]
        s = sc_ref[j]
        above = s >= top_f
        cnt = cnt + jnp.sum(by_sublane(jnp.where(above, 1, 0)), axis=0)
        below = by_sublane(jnp.where(above, -jnp.inf, s))
        for g in range(KEY_TILE // SUBLANES):
            lists[g % FINE_CHAINS] = insert(lists[g % FINE_CHAINS], below[g])
        return (cnt,) + tuple(v for lst in lists for v in lst)

    lowest = jnp.full((SUBLANES, Q_BLOCK), -jnp.inf, F32)
    carry = scan_tiles(fine_step, (jnp.zeros((SUBLANES, Q_BLOCK), I32),) + (lowest,) * (FINE_KEEP * FINE_CHAINS))
    n_above = jnp.sum(carry[0], axis=0, keepdims=True)
    best = list(carry[1:1 + FINE_KEEP])
    for c in range(1, FINE_CHAINS):
        best = merge(best, carry[1 + FINE_KEEP * c:1 + FINE_KEEP * (c + 1)])
    top = [b[0:1] for b in best]
    for sub in range(1, SUBLANES):
        top = merge(top, [b[sub:sub + 1] for b in best])
    r = TOPK_MAX - n_above
    thr_fast = top[FINE_KEEP - 1]
    for k in range(FINE_KEEP - 1, 0, -1):
        thr_fast = jnp.where(r == k, top[k - 1], thr_fast)
    found = (r <= FINE_KEEP) | jnp.logical_not(enough)
    resolved = jnp.min(jnp.where(found, 1, 0)) > 0

    def remaining_bits(_):
        return key_to_float(lax.fori_loop(COARSE_BITS, 32, bit_step, coarse))

    def many_in_range(_):
        bottom = key_to_float(coarse)
        at_bottom = found | (count_where(lambda s, base: s > bottom) < TOPK_MAX)
        value = jnp.where(found, thr_fast, bottom)
        return lax.cond(jnp.min(jnp.where(at_bottom, 1, 0)) > 0, lambda _: value, remaining_bits, 0)

    thr = lax.cond(resolved, lambda _: thr_fast, many_in_range, 0)
    thr = jnp.where(enough, thr, -jnp.inf)

    n_gt, n_eq = count_where(lambda s, base: s > thr, lambda s, base: s == thr)
    need = TOPK_MAX - n_gt
    seq_bits = 13

    def tie_search(_):
        def step(b, lo):
            cand = lo + lax.shift_left(jnp.int32(1), jnp.int32(seq_bits) - b)
            below = count_where(lambda s, base: (s == thr) & ((base + key_row) < cand))
            return jnp.where(below < need, cand, lo)
        return lax.fori_loop(0, seq_bits + 1, step, jnp.zeros((1, Q_BLOCK), I32))

    overflow = jnp.max(jnp.where(n_gt + n_eq > TOPK_MAX, 1, 0)) > 0
    last_tie = lax.cond(overflow, tie_search, lambda _: jnp.full((1, Q_BLOCK), 2 ** 30, I32), 0)

    def mask_bias(j):
        s = sc_ref[j]
        tie_ok = jnp.where((j * KEY_TILE + key_row) <= last_tie, 0.0, NEG_INF)
        mb = jnp.where(s > thr, 0.0, jnp.where(s == thr, tie_ok, NEG_INF))
        return jnp.transpose(mb)

    qlat = _dot(qa_ref[0], wuk_ref[...]) * (A_SCALE * LOG2E)
    lane_e = lax.broadcasted_iota(I32, (Q_BLOCK, LANES), 1)
    t0f = t0.astype(F32)
    qp = []
    for h in range(A_HEADS):
        sl = _alibi_slope(h) * LOG2E
        (c_hi, c_lo), (o_hi, o_lo) = _bf16_split(CHUNK * sl), _bf16_split(sl)
        cols = (c_hi, o_hi, -sl * t0f, c_lo, o_lo)
        qext = jnp.zeros((Q_BLOCK, LANES), F32)
        for n, v in enumerate(cols):
            qext = jnp.where(lane_e == n, v, qext)
        qp.append(jnp.concatenate([qlat[:, h * KV_RANK:(h + 1) * KV_RANK], qext], 1))
    qp = jnp.concatenate(qp, 0).astype(BF16)

    m_ref[...] = jnp.full(m_ref.shape, NEG_INF, F32)
    acc_ref[...] = jnp.zeros(acc_ref.shape, F32)

    def attend(j, slot, last=False):
        kt = key_rows(kp_ref, j)
        logits = _dot_nt(qp, kt)
        mb = mask_bias(j)
        if last:
            row_t = t0 + lax.broadcasted_iota(I32, (Q_BLOCK, KEY_TILE), 0)
            key_s = n_full * KEY_TILE + lax.broadcasted_iota(I32, (Q_BLOCK, KEY_TILE), 1)
            mb = jnp.where((key_s // CHUNK) <= (row_t // CHUNK), mb, NEG_INF)
            ahead = jnp.maximum(key_s - row_t, 0).astype(F32)
        alphas = []
        for h in range(A_HEADS):
            l = logits[h * Q_BLOCK:(h + 1) * Q_BLOCK] + mb
            if last:
                l = l - (2.0 * LOG2E * _alibi_slope(h)) * ahead
            m_old = m_ref[h]
            m_new = jnp.maximum(m_old, jnp.broadcast_to(jnp.max(l, axis=1, keepdims=True), (Q_BLOCK, LANES)))
            m_ref[h] = m_new
            alphas.append(jnp.exp2(m_old - m_new))
            p = jnp.exp2(l - jnp.concatenate([m_new] * (KEY_TILE // LANES), 1))
            p_ref[slot, h * Q_BLOCK:(h + 1) * Q_BLOCK, :] = p.astype(BF16)
        pv = _dot(p_ref[slot], kt)
        for h in range(A_HEADS):
            acc_ref[h] = acc_ref[h] * jnp.concatenate([alphas[h]] * 2, 1) + pv[h * Q_BLOCK:(h + 1) * Q_BLOCK]

    def attend_group(width, first):
        def body(jj, c):
            for k in range(width):
                attend(first + width * jj + k, k % 2)
            return c
        return body

    done = 0
    for width in ATTEND_UNROLLS:
        trips = (n_full - done) // width
        lax.fori_loop(0, trips, attend_group(width, done), 0)
        done = done + trips * width
    attend(n_full, 0, last=True)

    outs = []
    for h in range(A_HEADS):
        a = acc_ref[h]
        outs.append(a[:, :KV_RANK] / a[:, KV_RANK + 2:KV_RANK + 3])
    olat = jnp.concatenate(outs, 1).astype(BF16)
    o_ref[0] = _dot(olat, wuv_ref[...]).astype(BF16)


def _dsa_attention(qidx, widx, qa, krep, ckv, w_uk, w_uv, bsz, seq):
    nq = seq // Q_BLOCK
    n_key_tiles = seq // KEY_TILE
    pos = lax.broadcasted_iota(I32, (seq, LANES), 0)
    lane = lax.broadcasted_iota(I32, (seq, LANES), 1)
    ext = jnp.where((lane == 0) | (lane == 3), pos // CHUNK,
                    jnp.where((lane == 1) | (lane == 4), pos % CHUNK, jnp.where(lane == 2, 1, 0)))
    ext = ext.astype(BF16)
    eye = jnp.eye(A_HEADS, dtype=F32)
    wuk_bd = jnp.einsum("hg,hrd->hdgr", eye, w_uk).reshape(A_HEADS * A_HEAD_DIM, A_HEADS * KV_RANK)
    wuv_bd = jnp.einsum("hg,hrd->hrgd", eye, w_uv).reshape(A_HEADS * KV_RANK, A_HEADS * A_HEAD_DIM)
    blk = lambda b, i: (b, i, 0)
    whole = lambda b, i: (b, 0, 0)
    fixed = lambda b, i: (0, 0)
    r3 = lambda a: a.reshape(bsz, seq, a.shape[-1])
    return pl.pallas_call(
        _attn_kernel,
        grid=(bsz, nq),
        in_specs=[pl.BlockSpec((1, Q_BLOCK, 256), blk), pl.BlockSpec((1, Q_BLOCK, LANES), blk),
                  pl.BlockSpec((1, Q_BLOCK, 512), blk),
                  pl.BlockSpec((1, seq, 256), whole), pl.BlockSpec((1, seq, KV_RANK), whole),
                  pl.BlockSpec((seq, LANES), fixed),
                  pl.BlockSpec((512, 1024), fixed), pl.BlockSpec((1024, 512), fixed)],
        out_specs=pl.BlockSpec((1, Q_BLOCK, 512), blk),
        out_shape=jax.ShapeDtypeStruct((bsz, seq, 512), BF16),
        scratch_shapes=[pltpu.VMEM((seq, 2 * KV_RANK), BF16),
                        pltpu.VMEM((n_key_tiles, KEY_TILE, Q_BLOCK), F32),
                        pltpu.VMEM((2, A_HEADS * Q_BLOCK, KEY_TILE), BF16),
                        pltpu.VMEM((A_HEADS, Q_BLOCK, 2 * KV_RANK), F32),
                        pltpu.VMEM((A_HEADS, Q_BLOCK, LANES), F32)],
        compiler_params=_cparams(("arbitrary", "arbitrary")),
        name="attn",
    )(r3(qidx), r3(widx), r3(qa), r3(krep), r3(ckv), ext, wuk_bd.astype(BF16), wuv_bd.astype(BF16))


CONV_HALO = 32
CONV_PHASES = SUBLANES


def _conv_kernel(cur_ref, prev_ref, w_ref, b_ref, g_ref, beta_ref, o_ref, win_ref, shift_ref):
    j = pl.program_id(1)
    tq = cur_ref.shape[1]
    halo = prev_ref[0, tq - CONV_HALO:, :]
    win_ref[:CONV_HALO] = jnp.where(j > 0, halo, jnp.zeros_like(halo))
    win_ref[CONV_HALO:] = cur_ref[0]
    acc = jnp.zeros((tq, cur_ref.shape[2]), F32)
    first = CONV_HALO - (CONV_WIDTH - 1)
    for r in range(CONV_PHASES):
        taps = range(r, CONV_WIDTH, CONV_PHASES)
        rows = taps[-1] - r + tq
        shift_ref[r, :rows] = win_ref[first + r:first + r + rows, :]
        for k in taps:
            acc = acc + shift_ref[r, k - r:k - r + tq] * w_ref[k:k + 1, :]
    y = _layer_norm(acc + b_ref[...], g_ref[...], beta_ref[...])
    o_ref[0] = (y * jax.nn.sigmoid(y)).astype(BF16)


def _conv(u, w_dw, b_dw, ln_g, ln_b, bsz, seq, tq=512):
    c = u.shape[-1]
    u3 = u.reshape(bsz, seq, c)
    fixed = lambda b, j: (0, 0)
    return pl.pallas_call(
        _conv_kernel,
        grid=(bsz, seq // tq),
        in_specs=[pl.BlockSpec((1, tq, c), lambda b, j: (b, j, 0)),
                  pl.BlockSpec((1, tq, c), lambda b, j: (b, jnp.maximum(j - 1, 0), 0)),
                  pl.BlockSpec((CONV_WIDTH, c), fixed), pl.BlockSpec((1, c), fixed),
                  pl.BlockSpec((1, c), fixed), pl.BlockSpec((1, c), fixed)],
        out_specs=pl.BlockSpec((1, tq, c), lambda b, j: (b, j, 0)),
        out_shape=jax.ShapeDtypeStruct((bsz, seq, c), BF16),
        scratch_shapes=[pltpu.VMEM((tq + CONV_HALO, c), F32),
                        pltpu.VMEM((CONV_PHASES, tq + CONV_HALO, c), F32)],
        compiler_params=_cparams(("arbitrary", "arbitrary")),
        name="conv",
    )(u3, u3, w_dw, b_dw.reshape(1, c), ln_g.reshape(1, c), ln_b.reshape(1, c))


def _memattn_kernel(q_ref, mem_ref, wk_ref, wv_ref, o_ref, k_ref, v_ref):
    @pl.when(pl.program_id(1) == 0)
    def _():
        mb = mem_ref[0].astype(BF16)
        k_ref[...] = _dot(mb, wk_ref[...]).astype(BF16)
        v_ref[...] = _dot(mb, wv_ref[...]).astype(BF16)

    q = q_ref[0]
    outs = []
    for h in range(C_HEADS):
        sl = slice(h * C_HEAD_DIM, (h + 1) * C_HEAD_DIM)
        s = _dot_nt(q[:, sl], k_ref[:, sl]) * C_SCALE
        e = jnp.exp(s - jnp.max(s, axis=1, keepdims=True))
        p = e / jnp.sum(e, axis=1, keepdims=True)
        outs.append(_dot(p.astype(BF16), v_ref[:, sl]))
    o_ref[0] = jnp.concatenate(outs, 1).astype(BF16)


def _mem_attention(qc, mem, w_mem_k, w_mem_v, bsz, seq, tq=512):
    n_mem, dm = mem.shape[1:]
    cw = C_HEADS * C_HEAD_DIM
    fixed = lambda b, j: (0, 0)
    return pl.pallas_call(
        _memattn_kernel,
        grid=(bsz, seq // tq),
        in_specs=[pl.BlockSpec((1, tq, cw), lambda b, j: (b, j, 0)),
                  pl.BlockSpec((1, n_mem, dm), lambda b, j: (b, 0, 0)),
                  pl.BlockSpec((dm, cw), fixed), pl.BlockSpec((dm, cw), fixed)],
        out_specs=pl.BlockSpec((1, tq, cw), lambda b, j: (b, j, 0)),
        out_shape=jax.ShapeDtypeStruct((bsz, seq, cw), BF16),
        scratch_shapes=[pltpu.VMEM((n_mem, cw), BF16), pltpu.VMEM((n_mem, cw), BF16)],
        compiler_params=_cparams(("arbitrary", "arbitrary")),
        name="memattn",
    )(qc.reshape(bsz, seq, cw), mem, w_mem_k.astype(BF16), w_mem_v.astype(BF16))


def _mix_kernel(h0_ref, oa_ref, u2_ref, oc_ref, wg_ref, bg_ref, woa_ref, wpw_ref, bpw_ref, woc_ref,
                wout_ref, bout_ref, g_ref, b_ref, wr_ref, br_ref,
                h1_ref, eid_ref, gate_ref, rank_ref, cnt_ref, carry_ref):
    h0 = h0_ref[...]
    hb = h0.astype(BF16)
    dm = h0.shape[1]
    y = (_dot(oa_ref[...], woa_ref[...]),
         _dot(u2_ref[...], wpw_ref[...]) + bpw_ref[...],
         _dot(oc_ref[...], woc_ref[...]))
    mixed = jnp.zeros_like(h0)
    for k in range(N_BRANCH):
        gate = jax.nn.sigmoid(_dot(hb, wg_ref[:, k * dm:(k + 1) * dm]) + bg_ref[:, k * dm:(k + 1) * dm])
        mixed = mixed + gate * y[k]
    out = _dot(mixed.astype(BF16), wout_ref[...]) + bout_ref[...]
    h1 = _layer_norm(DN_ALPHA * h0 + out, g_ref[...], b_ref[...])
    h1_ref[...] = h1
    _route_tile(_dot(h1.astype(BF16), wr_ref[...]) + br_ref[...], eid_ref, gate_ref, rank_ref, cnt_ref, carry_ref)


def _mix(h0, oa, u2, oc, w_gate, b_gate, w_o_a, w_pw2, b_pw2, w_o_c, w_out, b_out, ln_g, ln_b, w_router, b_router,
         tm=256):
    n, dm = h0.shape
    row = lambda i: (i, 0)
    fixed = lambda i: (0, 0)
    wr = jnp.zeros((dm, LANES), F32).at[:, :N_EXPERTS].set(w_router).astype(BF16)
    br = jnp.zeros((1, LANES), F32).at[0, :N_EXPERTS].set(b_router)
    full = lambda a: pl.BlockSpec(a.shape, fixed)
    ws = [w_gate.astype(BF16), b_gate.reshape(1, -1), w_o_a.astype(BF16), w_pw2.astype(BF16), b_pw2.reshape(1, -1),
          w_o_c.astype(BF16), w_out.astype(BF16), b_out.reshape(1, -1), ln_g.reshape(1, -1), ln_b.reshape(1, -1),
          wr, br]
    return pl.pallas_call(
        _mix_kernel,
        grid=(n // tm,),
        in_specs=[pl.BlockSpec((tm, dm), row), pl.BlockSpec((tm, 512), row), pl.BlockSpec((tm, 512), row),
                  pl.BlockSpec((tm, 512), row)] + [full(a) for a in ws],
        out_specs=[pl.BlockSpec((tm, dm), row), pl.BlockSpec((ROUTE_ROWS, tm), lambda i: (0, i)),
                   pl.BlockSpec((tm, LANES), row), pl.BlockSpec((ROUTE_ROWS, tm), lambda i: (0, i)),
                   pl.BlockSpec((1, LANES), fixed)],
        out_shape=[jax.ShapeDtypeStruct((n, dm), F32), jax.ShapeDtypeStruct((ROUTE_ROWS, n), I32),
                   jax.ShapeDtypeStruct((n, LANES), F32), jax.ShapeDtypeStruct((ROUTE_ROWS, n), I32),
                   jax.ShapeDtypeStruct((1, LANES), F32)],
        scratch_shapes=[pltpu.VMEM((1, LANES), F32)],
        compiler_params=_cparams(("arbitrary",)),
        name="mix",
    )(h0, oa, u2, oc, *ws)


def _route_tile(rl, eid_ref, gate_ref, rank_ref, cnt_ref, carry_ref):
    @pl.when(pl.program_id(0) == 0)
    def _():
        carry_ref[...] = jnp.zeros_like(carry_ref)

    tm = rl.shape[0]
    lane = lax.broadcasted_iota(I32, (tm, LANES), 1)
    vals = jnp.where(lane < N_EXPERTS, rl, -jnp.inf)
    top_v, top_i, sels = [], [], []
    for _ in range(TOP_K):
        m = jnp.max(vals, axis=1, keepdims=True)
        idx = jnp.min(jnp.where(vals == m, lane, LANES), axis=1, keepdims=True)
        sel = lane == idx
        top_v.append(m)
        top_i.append(idx)
        sels.append(sel)
        vals = jnp.where(sel, -jnp.inf, vals)
    es = [jnp.exp(v - top_v[0]) for v in top_v]
    denom = es[0] + es[1] + es[2] + es[3]
    onehot = jnp.zeros((tm, LANES), F32)
    for sel in sels:
        onehot = onehot + jnp.where(sel, 1.0, 0.0)
    r = lax.broadcasted_iota(I32, (tm, tm), 0)
    c = lax.broadcasted_iota(I32, (tm, tm), 1)
    tri = jnp.where(c < r, 1.0, 0.0).astype(BF16)
    before = _dot(tri, onehot.astype(BF16)) + carry_ref[...]
    carry = carry_ref[...] + jnp.sum(onehot, axis=0, keepdims=True)
    carry_ref[...] = carry
    cnt_ref[...] = carry
    eid = jnp.zeros((tm, LANES), I32)
    gate = jnp.zeros((tm, LANES), F32)
    rank = jnp.zeros((tm, LANES), F32)
    for k in range(TOP_K):
        eid = jnp.where(lane == k, top_i[k], eid)
        gate = jnp.where(lane == k, es[k] / denom, gate)
        rank = jnp.where(lane == k, jnp.sum(jnp.where(sels[k], before, 0.0), axis=1, keepdims=True), rank)
    eid_ref[...] = jnp.transpose(eid)[:ROUTE_ROWS]
    gate_ref[...] = gate
    rank_ref[...] = jnp.transpose(rank.astype(I32))[:ROUTE_ROWS]


ROUTE_ROWS = SUBLANES


DMA_UNROLL = 4


def _to_tiles(x, tile_ref):
    for s in range(SUBLANES):
        tile_ref[pl.ds(s, x.shape[0], stride=SUBLANES), :] = x[:, s * LANES:(s + 1) * LANES]


def _from_tiles(tile_ref):
    rows = tile_ref.shape[0] // SUBLANES
    return jnp.concatenate([tile_ref[pl.ds(s, rows, stride=SUBLANES), :] for s in range(SUBLANES)], axis=1)


def _tile_rows(ref, row):
    return ref.at[pl.ds(pl.multiple_of(row * SUBLANES, SUBLANES), SUBLANES)]


def _dispatch_kernel(dest_ref, h1_ref, xs_in_hbm, xs_hbm, stage_ref, sem):
    del xs_in_hbm
    tm = h1_ref.shape[0]
    _to_tiles(h1_ref[...], stage_ref)

    def start(t, c):
        for k in range(TOP_K):
            pltpu.make_async_copy(_tile_rows(stage_ref, t), _tile_rows(xs_hbm, dest_ref[k, t]),
                                  sem).start(priority=k % 2)
        return c

    def wait(t, c):
        for k in range(TOP_K):
            pltpu.make_async_copy(_tile_rows(stage_ref, 0), _tile_rows(xs_hbm, 0), sem).wait()
        return c

    lax.fori_loop(0, tm, start, 0, unroll=DMA_UNROLL)
    lax.fori_loop(0, tm, wait, 0, unroll=DMA_UNROLL)


def _dispatch(h1, dest_t, n_rows, tm=256):
    n, dm = h1.shape
    xs0 = jnp.zeros((n_rows * SUBLANES, LANES), F32)
    return pl.pallas_call(
        _dispatch_kernel,
        grid=(n // tm,),
        in_specs=[pl.BlockSpec((ROUTE_ROWS, tm), lambda i: (0, i), memory_space=pltpu.SMEM),
                  pl.BlockSpec((tm, dm), lambda i: (i, 0)), pl.BlockSpec(memory_space=pl.ANY)],
        out_specs=pl.BlockSpec(memory_space=pl.ANY),
        out_shape=jax.ShapeDtypeStruct((n_rows * SUBLANES, LANES), F32),
        scratch_shapes=[pltpu.VMEM((tm * SUBLANES, LANES), F32), pltpu.SemaphoreType.DMA(())],
        input_output_aliases={2: 0},
        compiler_params=_cparams(("arbitrary",)),
        name="dispatch",
    )(dest_t, h1, xs0)


def _experts_kernel(blk_exp_ref, n_used_ref, xs_ref, w1_ref, b1_ref, w2_ref, b2_ref, ys_ref, w1b_ref, w2b_ref):
    b = pl.program_id(0)
    prev = blk_exp_ref[jnp.maximum(b - 1, 0)]
    used = b < n_used_ref[0]

    @pl.when(used & ((b == 0) | (blk_exp_ref[b] != prev)))
    def _():
        w1b_ref[...] = w1_ref[0].astype(BF16)
        w2b_ref[...] = w2_ref[0].astype(BF16)

    @pl.when(used)
    def _():
        d_exp = w2b_ref.shape[0]
        hdn = _dot(_from_tiles(xs_ref).astype(BF16), w1b_ref[...]) + b1_ref[0]
        g = jnp.minimum(hdn[:, :d_exp], SWIGLU_LIMIT)
        u = jnp.clip(hdn[:, d_exp:], -SWIGLU_LIMIT, SWIGLU_LIMIT)
        act = (u + 1.0) * (g * jax.nn.sigmoid(SWIGLU_ALPHA * g))
        _to_tiles(_dot(act.astype(BF16), w2b_ref[...]) + b2_ref[0], ys_ref)

    @pl.when(jnp.logical_not(used))
    def _():
        ys_ref[...] = jnp.zeros_like(ys_ref)


def _experts(xs, blk_exp, n_used, w1, b1, w2, b2):
    n_rows = xs.shape[0] // SUBLANES
    n_exp, dm, two_f = w1.shape
    d_exp = two_f // 2
    grid_spec = pltpu.PrefetchScalarGridSpec(
        num_scalar_prefetch=2,
        grid=(n_rows // ROW_BLOCK,),
        in_specs=[pl.BlockSpec((ROW_BLOCK * SUBLANES, LANES), lambda b, be, nu: (b, 0)),
                  pl.BlockSpec((1, dm, two_f), lambda b, be, nu: (be[b], 0, 0)),
                  pl.BlockSpec((1, 1, two_f), lambda b, be, nu: (be[b], 0, 0)),
                  pl.BlockSpec((1, d_exp, dm), lambda b, be, nu: (be[b], 0, 0)),
                  pl.BlockSpec((1, 1, dm), lambda b, be, nu: (be[b], 0, 0))],
        out_specs=pl.BlockSpec((ROW_BLOCK * SUBLANES, LANES), lambda b, be, nu: (b, 0)),
        scratch_shapes=[pltpu.VMEM((dm, two_f), BF16), pltpu.VMEM((d_exp, dm), BF16)],
    )
    return pl.pallas_call(
        _experts_kernel,
        grid_spec=grid_spec,
        out_shape=jax.ShapeDtypeStruct((n_rows * SUBLANES, LANES), F32),
        compiler_params=_cparams(("arbitrary",)),
        name="experts",
    )(blk_exp, n_used, xs, w1, b1.reshape(n_exp, 1, two_f), w2, b2.reshape(n_exp, 1, dm))


def _combine_kernel(dest_ref, ys_hbm, h1_ref, gate_ref, g_ref, b_ref, o_ref, buf_ref, sem):
    tm = h1_ref.shape[0]

    def start(t, c):
        for k in range(TOP_K):
            pltpu.make_async_copy(_tile_rows(ys_hbm, dest_ref[k, t]), _tile_rows(buf_ref.at[k], t),
                                  sem).start(priority=k % 2)
        return c

    def wait(t, c):
        for k in range(TOP_K):
            pltpu.make_async_copy(_tile_rows(ys_hbm, 0), _tile_rows(buf_ref.at[0], 0), sem).wait()
        return c

    lax.fori_loop(0, tm, start, 0, unroll=DMA_UNROLL)
    lax.fori_loop(0, tm, wait, 0, unroll=DMA_UNROLL)
    gate = gate_ref[...]
    moe = jnp.zeros(h1_ref.shape, F32)
    for k in range(TOP_K):
        moe = moe + _from_tiles(buf_ref.at[k]) * gate[:, k:k + 1]
    o_ref[...] = _layer_norm(DN_ALPHA * h1_ref[...] + moe, g_ref[...], b_ref[...])


def _combine(ys, dest_t, h1, gate, ln_g, ln_b, tm=256):
    n, dm = h1.shape
    row = lambda i: (i, 0)
    fixed = lambda i: (0, 0)
    return pl.pallas_call(
        _combine_kernel,
        grid=(n // tm,),
        in_specs=[pl.BlockSpec((ROUTE_ROWS, tm), lambda i: (0, i), memory_space=pltpu.SMEM),
                  pl.BlockSpec(memory_space=pl.ANY),
                  pl.BlockSpec((tm, dm), row), pl.BlockSpec((tm, LANES), row),
                  pl.BlockSpec((1, dm), fixed), pl.BlockSpec((1, dm), fixed)],
        out_specs=pl.BlockSpec((tm, dm), row),
        out_shape=jax.ShapeDtypeStruct((n, dm), F32),
        scratch_shapes=[pltpu.VMEM((TOP_K, tm * SUBLANES, LANES), F32), pltpu.SemaphoreType.DMA(())],
        compiler_params=_cparams(("arbitrary",)),
        name="combine",
    )(dest_t, ys, h1, gate, ln_g.reshape(1, dm), ln_b.reshape(1, dm))


def _moe(h1, routing, w1, b1, w2, b2, ln_g, ln_b):
    n, dm = h1.shape
    eid_t, gate, rank_t, counts = routing
    counts = counts[0, :N_EXPERTS].astype(I32)
    padded = (counts + ROW_BLOCK - 1) // ROW_BLOCK * ROW_BLOCK
    pad_end = jnp.cumsum(padded)
    group_start = pad_end - padded
    n_blocks = n * TOP_K // ROW_BLOCK + N_EXPERTS
    n_used = (pad_end[-1] // ROW_BLOCK).astype(I32)
    blk = jnp.minimum(jnp.arange(n_blocks, dtype=I32), n_used - 1)
    blk_exp = jnp.sum((pad_end[None, :] <= (blk * ROW_BLOCK)[:, None]).astype(I32), axis=1)
    expert = jnp.arange(N_EXPERTS, dtype=I32)[:, None, None]
    dest_t = jnp.sum(jnp.where(eid_t[None] == expert, group_start[:, None, None], 0), axis=0) + rank_t
    xs = _dispatch(h1, dest_t, n_blocks * ROW_BLOCK)
    ys = _experts(xs, blk_exp, n_used.reshape(1), w1, b1, w2, b2)
    return _combine(ys, dest_t, h1, gate, ln_g, ln_b)


def kernel(x, mem, ln0_g, ln0_b, w_in, b_in, kv_norm_g, w_uk, w_uv, w_o_a, w_dw, b_dw, conv_ln_g, conv_ln_b, w_pw2, b_pw2, w_mem_k, w_mem_v, w_o_c, w_out, b_out, ln1_g, ln1_b, w_router, b_router, w1, b1, w2, b2, ln2_g, ln2_b):
    bsz, seq, dm = x.shape
    n = bsz * seq
    assert w_in.shape[0] == DEPTH == 1, "the stem LayerNorm is fused into the single layer's projection"
    l = 0
    gate_lo = w_in.shape[-1] - N_BRANCH * dm
    h0, qa, ckv, qidx, krep, widx, u, qc = _proj(x.reshape(n, dm), ln0_g, ln0_b, w_in[l, :, :gate_lo],
                                                   b_in[l, :gate_lo], kv_norm_g[l])
    oa = _dsa_attention(qidx, widx, qa, krep, ckv, w_uk[l], w_uv[l], bsz, seq).reshape(n, -1)
    u2 = _conv(u, w_dw[l], b_dw[l], conv_ln_g[l], conv_ln_b[l], bsz, seq).reshape(n, -1)
    oc = _mem_attention(qc, mem, w_mem_k[l], w_mem_v[l], bsz, seq).reshape(n, -1)
    h1, *routing = _mix(h0, oa, u2, oc, w_in[l, :, gate_lo:], b_in[l, gate_lo:], w_o_a[l], w_pw2[l], b_pw2[l],
                        w_o_c[l], w_out[l], b_out[l], ln1_g[l], ln1_b[l], w_router[l], b_router[l])
    h2 = _moe(h1, routing, w1[l], b1[l], w2[l], b2[l], ln2_g[l], ln2_b[l])
    return h2.reshape(bsz, seq, dm)
```
